```python
import jax, jax.numpy as jnp
from jax import lax
import numpy as np

D_MODEL = 1024
BATCH = 8
SEQ = 2048
DEPTH = 2
DEC_BATCH = 128
DEC_SEQ = 4
PAST_LEN = 16384
PAGE_SIZE = 128

N_MIXERS = 2
N_LRU_LAYERS = (DEPTH + 1) // 2
N_CONF_LAYERS = DEPTH // 2
D_RNN = (D_MODEL * 5) // 4
N_RNN_HEADS = 10
RNN_HEAD_DIM = D_RNN // N_RNN_HEADS
LRU_CONV_W = 4
LRU_C = 8.0
D_CONF = D_MODEL
CONF_CONV_W = 31
N_PEER_HEADS = 8
N_KEYS = 128
N_EXPERTS = N_KEYS * N_KEYS
D_KEY = 256
HALF_KEY = D_KEY // 2
TOPK_HALF = 16
TOPK = 16
PEER_BLOCK = 256
DEEPNORM_ALPHA = (2 * DEPTH) ** 0.25
DEEPNORM_BETA = (8 * DEPTH) ** -0.25
LN_EPS = 1e-5

kernel_name = "hybrid_rglru_conformer_peer_step"


def layer_norm(x, g, b):
    xf = x.astype(jnp.float32)
    mu = jnp.mean(xf, axis=-1, keepdims=True)
    var = jnp.mean(jnp.square(xf - mu), axis=-1, keepdims=True)
    return ((xf - mu) * lax.rsqrt(var + LN_EPS)).astype(x.dtype) * g + b


def causal_depthwise_conv(buf, x, w, b):
    xp = jnp.concatenate([buf.astype(x.dtype), x], axis=1)
    c = x.shape[-1]
    y = lax.conv_general_dilated(xp, w[:, None, :].astype(x.dtype), (1,), 'VALID',
                                 dimension_numbers=('NWC', 'WIO', 'NWC'),
                                 feature_group_count=c) + b
    return y, xp[:, -(w.shape[0] - 1):]


def rg_lru(xc, h0, w_a, b_a, w_i, b_i, lam):
    bsz, L, _ = xc.shape
    xh = xc.reshape(bsz, L, N_RNN_HEADS, RNN_HEAD_DIM)
    r = jax.nn.sigmoid(jnp.einsum('blhi,hij->blhj', xh, w_a).reshape(bsz, L, D_RNN) + b_a)
    ig = jax.nn.sigmoid(jnp.einsum('blhi,hij->blhj', xh, w_i).reshape(bsz, L, D_RNN) + b_i)
    log_a = (-LRU_C * r.astype(jnp.float32)) * jax.nn.softplus(-lam.astype(jnp.float32))
    a = jnp.exp(log_a)
    bterm = jnp.sqrt(-jnp.expm1(2.0 * log_a)) * (ig * xc).astype(jnp.float32)

    def combine(left, right):
        a1, b1 = left
        a2, b2 = right
        return a1 * a2, a2 * b1 + b2

    a_cum, b_cum = lax.associative_scan(combine, (a, bterm), axis=1)
    h = a_cum * h0.astype(jnp.float32)[:, None, :] + b_cum
    return h.astype(xc.dtype), h[:, -1].astype(h0.dtype)


def lru_mixer(h_in, h0, conv0, w_in, conv_w, conv_b, w_a, b_a, w_i, b_i, lam, w_out):
    proj = h_in @ w_in
    xb, gb = proj[..., :D_RNN], proj[..., D_RNN:]
    xc, conv_new = causal_depthwise_conv(conv0, xb, conv_w, conv_b)
    y, h_new = rg_lru(xc, h0, w_a, b_a, w_i, b_i, lam)
    out = (y * jax.nn.gelu(gb, approximate=False)) @ w_out
    return out, h_new, conv_new


def conformer_mixer(h_in, buf0, w_pw1, b_pw1, dw_w, dw_b, ln_g, ln_b, w_pw2, b_pw2):
    p = h_in @ w_pw1 + b_pw1
    u = p[..., :D_CONF] * jax.nn.sigmoid(p[..., D_CONF:])
    d, buf_new = causal_depthwise_conv(buf0, u, dw_w, dw_b)
    d = jax.nn.silu(layer_norm(d, ln_g, ln_b))
    return d @ w_pw2 + b_pw2, buf_new


def peer(xm, w_q, keys, u_tab, v_tab):
    T, D = xm.shape
    nb = -(-T // PEER_BLOCK)
    xs = jnp.pad(xm, ((0, nb * PEER_BLOCK - T), (0, 0))).reshape(nb, PEER_BLOCK, D)

    def block(xb):
        q = (xb @ w_q).reshape(PEER_BLOCK, N_PEER_HEADS, 2, HALF_KEY)
        s = jnp.einsum('thpd,hpnd->thpn', q, keys).astype(jnp.float32)
        v1, i1 = lax.top_k(s[:, :, 0], TOPK_HALF)
        v2, i2 = lax.top_k(s[:, :, 1], TOPK_HALF)
        cand = (v1[..., :, None] + v2[..., None, :]).reshape(PEER_BLOCK, N_PEER_HEADS, TOPK_HALF * TOPK_HALF)
        cand_idx = (i1[..., :, None] * N_KEYS + i2[..., None, :]).reshape(PEER_BLOCK, N_PEER_HEADS, TOPK_HALF * TOPK_HALF)
        top_s, pos = lax.top_k(cand, TOPK)
        expert = jnp.take_along_axis(cand_idx, pos, axis=-1)
        g = jax.nn.softmax(top_s, axis=-1).astype(xb.dtype)
        u = u_tab[expert]
        v = v_tab[expert]
        act = jax.nn.gelu(jnp.einsum('td,thkd->thk', xb, u), approximate=False)
        return jnp.einsum('thk,thkd->td', g * act, v)

    return lax.map(block, xs).reshape(nb * PEER_BLOCK, D)[:T]


def trunk(x, c, lru_h0, lru_conv0, conf_conv0, p):
    bsz, L, D = x.shape
    new_h, new_lconv, new_cconv = [], [], []
    for i in range(DEPTH):
        mod = jax.nn.silu(c) @ p['w_ada'][i] + p['b_ada'][i]
        sh_m, sc_m, g_m, sh_f, sc_f, g_f = jnp.split(mod[:, None, :], 6, axis=-1)
        hm = x * (1 + sc_m) + sh_m
        j = i // N_MIXERS
        if i % N_MIXERS == 0:
            out, hn, cn = lru_mixer(hm, lru_h0[j], lru_conv0[j], p['lru_w_in'][j], p['lru_conv_w'][j],
                                    p['lru_conv_b'][j], p['lru_w_a'][j], p['lru_b_a'][j], p['lru_w_i'][j],
                                    p['lru_b_i'][j], p['lru_lambda'][j], p['lru_w_out'][j])
            new_h.append(hn)
            new_lconv.append(cn)
        else:
            out, cn = conformer_mixer(hm, conf_conv0[j], p['conf_w_pw1'][j], p['conf_b_pw1'][j],
                                      p['conf_dw_w'][j], p['conf_dw_b'][j], p['conf_ln_g'][j],
                                      p['conf_ln_b'][j], p['conf_w_pw2'][j], p['conf_b_pw2'][j])
            new_cconv.append(cn)
        x = layer_norm(DEEPNORM_ALPHA * x + g_m * out, p['ln_mix_g'][i], p['ln_mix_b'][i])
        hf = x * (1 + sc_f) + sh_f
        f = peer(hf.reshape(bsz * L, D), p['peer_w_q'][i], p['peer_keys'][i],
                 p['peer_u'][i], p['peer_v'][i]).reshape(bsz, L, D)
        x = layer_norm(DEEPNORM_ALPHA * x + g_f * f, p['ln_ffn_g'][i], p['ln_ffn_b'][i])
    return x, jnp.stack(new_h), jnp.stack(new_lconv), jnp.stack(new_cconv)


def setup_inputs(seed: int = 0) -> dict:
    key = jax.random.key(seed)
    ks = jax.random.split(key, 40)
    nrm = lambda k, shape, s: jax.random.normal(k, shape, jnp.float32) * s
    D = D_MODEL
    a0 = jax.random.uniform(ks[20], (N_LRU_LAYERS, D_RNN), jnp.float32, 0.9, 0.999)
    p0 = a0 ** (1.0 / LRU_C)
    return {
        'x_prompt': nrm(ks[0], (BATCH, SEQ, D), 1.0),
        'x_sample': nrm(ks[1], (DEC_BATCH, DEC_SEQ, D), 1.0),
        'state_lru_h': nrm(ks[2], (N_LRU_LAYERS, DEC_BATCH, D_RNN), 0.5),
        'state_lru_conv': nrm(ks[3], (N_LRU_LAYERS, DEC_BATCH, LRU_CONV_W - 1, D_RNN), 1.0),
        'state_conf_conv': nrm(ks[4], (N_CONF_LAYERS, DEC_BATCH, CONF_CONV_W - 1, D_CONF), 0.5),
        'c_prompt': nrm(ks[5], (BATCH, D), 1.0),
        'c_sample': nrm(ks[6], (DEC_BATCH, D), 1.0),
        'w_ada': nrm(ks[7], (DEPTH, D, 6 * D), 0.5 * D ** -0.5),
        'b_ada': nrm(ks[8], (DEPTH, 6 * D), 0.01),
        'ln_mix_g': 1.0 + nrm(ks[9], (DEPTH, D), 0.05),
        'ln_mix_b': nrm(ks[10], (DEPTH, D), 0.01),
        'ln_ffn_g': 1.0 + nrm(ks[11], (DEPTH, D), 0.05),
        'ln_ffn_b': nrm(ks[12], (DEPTH, D), 0.01),
        'lru_w_in': nrm(ks[13], (N_LRU_LAYERS, D, 2 * D_RNN), D ** -0.5),
        'lru_conv_w': nrm(ks[14], (N_LRU_LAYERS, LRU_CONV_W, D_RNN), LRU_CONV_W ** -0.5),
        'lru_conv_b': nrm(ks[15], (N_LRU_LAYERS, D_RNN), 0.01),
        'lru_w_a': nrm(ks[16], (N_LRU_LAYERS, N_RNN_HEADS, RNN_HEAD_DIM, RNN_HEAD_DIM), RNN_HEAD_DIM ** -0.5),
        'lru_b_a': nrm(ks[17], (N_LRU_LAYERS, D_RNN), 0.01),
        'lru_w_i': nrm(ks[18], (N_LRU_LAYERS, N_RNN_HEADS, RNN_HEAD_DIM, RNN_HEAD_DIM), RNN_HEAD_DIM ** -0.5),
        'lru_b_i': nrm(ks[19], (N_LRU_LAYERS, D_RNN), 0.01),
        'lru_lambda': jnp.log(p0) - jnp.log1p(-p0),
        'lru_w_out': nrm(ks[21], (N_LRU_LAYERS, D_RNN, D), DEEPNORM_BETA * D_RNN ** -0.5),
        'conf_w_pw1': nrm(ks[22], (N_CONF_LAYERS, D, 2 * D_CONF), D ** -0.5),
        'conf_b_pw1': nrm(ks[23], (N_CONF_LAYERS, 2 * D_CONF), 0.01),
        'conf_dw_w': nrm(ks[24], (N_CONF_LAYERS, CONF_CONV_W, D_CONF), CONF_CONV_W ** -0.5),
        'conf_dw_b': nrm(ks[25], (N_CONF_LAYERS, D_CONF), 0.01),
        'conf_ln_g': 1.0 + nrm(ks[26], (N_CONF_LAYERS, D_CONF), 0.05),
        'conf_ln_b': nrm(ks[27], (N_CONF_LAYERS, D_CONF), 0.01),
        'conf_w_pw2': nrm(ks[28], (N_CONF_LAYERS, D_CONF, D), DEEPNORM_BETA * D_CONF ** -0.5),
        'conf_b_pw2': nrm(ks[29], (N_CONF_LAYERS, D), 0.01),
        'peer_w_q': nrm(ks[30], (DEPTH, D, N_PEER_HEADS * D_KEY), D ** -0.5),
        'peer_keys': nrm(ks[31], (DEPTH, N_PEER_HEADS, 2, N_KEYS, HALF_KEY), HALF_KEY ** -0.5),
        'peer_u': nrm(ks[32], (DEPTH, N_EXPERTS, D), D ** -0.5),
        'peer_v': nrm(ks[33], (DEPTH, N_EXPERTS, D), 0.5 * DEEPNORM_BETA),
    }


def reference(x_prompt, x_sample, state_lru_h, state_lru_conv, state_conf_conv, c_prompt, c_sample,
              w_ada, b_ada, ln_mix_g, ln_mix_b, ln_ffn_g, ln_ffn_b,
              lru_w_in, lru_conv_w, lru_conv_b, lru_w_a, lru_b_a, lru_w_i, lru_b_i, lru_lambda, lru_w_out,
              conf_w_pw1, conf_b_pw1, conf_dw_w, conf_dw_b, conf_ln_g, conf_ln_b, conf_w_pw2, conf_b_pw2,
              peer_w_q, peer_keys, peer_u, peer_v):
    params = {
        'w_ada': w_ada, 'b_ada': b_ada, 'ln_mix_g': ln_mix_g, 'ln_mix_b': ln_mix_b,
        'ln_ffn_g': ln_ffn_g, 'ln_ffn_b': ln_ffn_b,
        'lru_w_in': lru_w_in, 'lru_conv_w': lru_conv_w, 'lru_conv_b': lru_conv_b,
        'lru_w_a': lru_w_a, 'lru_b_a': lru_b_a, 'lru_w_i': lru_w_i, 'lru_b_i': lru_b_i,
        'lru_lambda': lru_lambda, 'lru_w_out': lru_w_out,
        'conf_w_pw1': conf_w_pw1, 'conf_b_pw1': conf_b_pw1, 'conf_dw_w': conf_dw_w, 'conf_dw_b': conf_dw_b,
        'conf_ln_g': conf_ln_g, 'conf_ln_b': conf_ln_b, 'conf_w_pw2': conf_w_pw2, 'conf_b_pw2': conf_b_pw2,
        'peer_w_q': peer_w_q, 'peer_keys': peer_keys, 'peer_u': peer_u, 'peer_v': peer_v,
    }
    bsz = x_prompt.shape[0]
    dt = x_prompt.dtype
    h0_p = jnp.zeros((N_LRU_LAYERS, bsz, D_RNN), dt)
    lconv0_p = jnp.zeros((N_LRU_LAYERS, bsz, LRU_CONV_W - 1, D_RNN), dt)
    cconv0_p = jnp.zeros((N_CONF_LAYERS, bsz, CONF_CONV_W - 1, D_CONF), dt)
    y_prompt, h_p, lconv_p, cconv_p = trunk(x_prompt, c_prompt, h0_p, lconv0_p, cconv0_p, params)
    y_sample, h_s, lconv_s, cconv_s = trunk(x_sample, c_sample, state_lru_h, state_lru_conv,
                                            state_conf_conv, params)
    return (y_prompt, y_sample, h_p, lconv_p, cconv_p, h_s, lconv_s, cconv_s)
```

```python
import functools
import math

import jax
import jax.numpy as jnp
from jax import lax
from jax.experimental import pallas as pl
from jax.experimental.pallas import tpu as pltpu

F32 = jnp.float32
BF16 = jnp.bfloat16

D_MODEL = 1024
DEPTH = 2
D_RNN = 1280
N_RNN_HEADS = 10
RNN_HEAD_DIM = 128
LRU_CONV_W = 4
LRU_C = 8.0
D_CONF = 1024
CONF_CONV_W = 31
N_PEER_HEADS = 8
N_KEYS = 128
N_EXPERTS = N_KEYS * N_KEYS
HALF_KEY = 128
TOPK = 16
N_SEL = N_PEER_HEADS * TOPK
DEEPNORM_ALPHA = (2 * DEPTH) ** 0.25
LN_EPS = 1e-5

LANES = 128
GROUP = 128
TM = 512
TT = 128
WORDS = D_MODEL // 2
SLABS = WORDS // LANES
STAGE_STRIDE = 136
N_STAGE = 4
VMEM_LIMIT = 56 * 1024 * 1024

_NT = (((1,), (1,)), ((), ()))


def _cparams(n_axes, vmem=VMEM_LIMIT):
    return pltpu.CompilerParams(dimension_semantics=("arbitrary",) * n_axes, vmem_limit_bytes=vmem)


def _gelu(x):
    return 0.5 * x * (1.0 + lax.erf(x * (1.0 / math.sqrt(2.0))))


def _affine(x, scale, shift=None):
    rows, c = x.shape
    xr = x.reshape(rows // GROUP, GROUP, c) * scale[None]
    if shift is not None:
        xr = xr + shift[None]
    return xr.reshape(rows, c)


def _layer_norm(z, g, b):
    mu = jnp.mean(z, axis=-1, keepdims=True)
    zc = z - mu
    var = jnp.mean(zc * zc, axis=-1, keepdims=True)
    return zc * lax.rsqrt(var + LN_EPS) * g + b


def _mod_kernel(c_ref, w_ref, b_ref, o_ref):
    c = c_ref[...]
    a = (c * jax.nn.sigmoid(c)).astype(BF16)
    o_ref[0] = jnp.dot(a, w_ref[0].astype(BF16), preferred_element_type=F32) + b_ref[0]


def _mod_call(c_all, w_ada, b_ada):
    nb = c_all.shape[0]
    tn = 1536
    return pl.pallas_call(
        _mod_kernel,
        grid=(DEPTH, 6 * D_MODEL // tn),
        in_specs=[
            pl.BlockSpec((nb, D_MODEL), lambda l, j: (0, 0)),
            pl.BlockSpec((1, D_MODEL, tn), lambda l, j: (l, 0, j)),
            pl.BlockSpec((1, 1, tn), lambda l, j: (l, 0, j)),
        ],
        out_specs=pl.BlockSpec((1, nb, tn), lambda l, j: (l, 0, j)),
        out_shape=jax.ShapeDtypeStruct((DEPTH, nb, 6 * D_MODEL), F32),
        compiler_params=_cparams(2),
        name="adaln_mod",
    )(c_all, w_ada, b_ada.reshape(DEPTH, 1, 6 * D_MODEL))


def _mm_mod_kernel(x_ref, sc_ref, sh_ref, w_ref, *rest, glu):
    o_ref = rest[-1]
    xm = _affine(x_ref[...], 1.0 + sc_ref[0], sh_ref[0])
    p = jnp.dot(xm.astype(BF16), w_ref[...], preferred_element_type=F32)
    if len(rest) == 2:
        p = p + rest[0][...]
    if glu:
        n = p.shape[-1] // 2
        p = p[:, :n] * jax.nn.sigmoid(p[:, n:])
    o_ref[...] = p


def _mm_mod_call(x, sc, sh, w_bf16, bias, n_prompt_tiles, glu, name):
    t = x.shape[0]
    n = w_bf16.shape[1]
    n_out = n // 2 if glu else n
    sel = lambda i: (jnp.where(i >= n_prompt_tiles, 1, 0), 0, 0)
    in_specs = [
        pl.BlockSpec((TM, D_MODEL), lambda i: (i, 0)),
        pl.BlockSpec((1, GROUP, D_MODEL), sel),
        pl.BlockSpec((1, GROUP, D_MODEL), sel),
        pl.BlockSpec((D_MODEL, n), lambda i: (0, 0)),
    ]
    args = [x, sc, sh, w_bf16]
    if bias is not None:
        in_specs.append(pl.BlockSpec((1, n), lambda i: (0, 0)))
        args.append(bias.reshape(1, n))
    return pl.pallas_call(
        functools.partial(_mm_mod_kernel, glu=glu),
        grid=(t // TM,),
        in_specs=in_specs,
        out_specs=pl.BlockSpec((TM, n_out), lambda i: (i, 0)),
        out_shape=jax.ShapeDtypeStruct((t, n_out), F32),
        compiler_params=_cparams(1),
        name=name,
    )(*args)


def _lru_kernel(proj_ref, conv0_ref, h0_ref, cw_ref, cb_ref, wa_ref, ba_ref, wi_ref, bi_ref, lam_ref,
                yg_ref, hnew_ref, convnew_ref, win_ref, xc_ref, a_ref, b_ref, h_ref, *, nb, tl, cw, n_tiles):
    rows = tl * nb
    halo = (LRU_CONV_W - 1) * nb

    @pl.when(pl.program_id(0) == 0)
    def _():
        win_ref[0:halo, :] = conv0_ref[...]
        h_ref[...] = h0_ref[...]

    win_ref[halo:halo + rows, :] = proj_ref[:, :D_RNN]
    xc = cb_ref[...] + win_ref[0:rows, :] * cw_ref[0:1, :]
    for w in range(1, LRU_CONV_W):
        xc = xc + win_ref[w * nb:w * nb + rows, :] * cw_ref[w:w + 1, :]
    xc_ref[...] = xc
    carry = win_ref[rows:rows + halo, :]
    convnew_ref[...] = carry
    if n_tiles > 1:
        win_ref[0:halo, :] = carry

    lam = lam_ref[...]
    neg = -lam
    softplus = jnp.maximum(neg, 0.0) + jnp.log1p(jnp.exp(-jnp.abs(neg)))
    for hd in range(N_RNN_HEADS):
        cs = slice(hd * RNN_HEAD_DIM, (hd + 1) * RNN_HEAD_DIM)
        xh = xc_ref[:, cs]
        xb = xh.astype(BF16)
        r = jax.nn.sigmoid(jnp.dot(xb, wa_ref[hd], preferred_element_type=F32) + ba_ref[:, cs])
        ig = jax.nn.sigmoid(jnp.dot(xb, wi_ref[hd], preferred_element_type=F32) + bi_ref[:, cs])
        log_a = (-LRU_C * r) * softplus[:, cs]
        th = jnp.tanh(log_a)
        a_ref[:, cs] = jnp.exp(log_a)
        b_ref[:, cs] = jnp.sqrt(-2.0 * th / (1.0 - th)) * (ig * xh)

    for c0 in range(0, D_RNN, cw):
        cs = slice(c0, c0 + cw)

        def step(l, h, cs=cs):
            rs = pl.ds(pl.multiple_of(l * nb, nb), nb)
            h = a_ref[rs, cs] * h + b_ref[rs, cs]
            b_ref[rs, cs] = h
            return h

        h_ref[:, cs] = lax.fori_loop(0, tl, step, h_ref[:, cs])

    yg_ref[...] = b_ref[...] * _gelu(proj_ref[:, D_RNN:])
    hnew_ref[...] = h_ref[...]


def _lru_call(proj, conv0, h0, cw, cb, wa, ba, wi, bi, lam, *, nb, seq, row_block0, name):
    tl = TM // nb
    n_tiles = seq // tl
    halo = (LRU_CONV_W - 1) * nb
    chan = D_RNN if nb * D_RNN <= 16 * 1024 else LANES
    full = lambda shape: pl.BlockSpec(shape, lambda i: (0,) * len(shape))
    return pl.pallas_call(
        functools.partial(_lru_kernel, nb=nb, tl=tl, cw=chan, n_tiles=n_tiles),
        grid=(n_tiles,),
        in_specs=[
            pl.BlockSpec((TM, 2 * D_RNN), lambda i: (row_block0 + i, 0)),
            full((halo, D_RNN)),
            full((nb, D_RNN)),
            full((LRU_CONV_W, D_RNN)),
            full((1, D_RNN)),
            full((N_RNN_HEADS, RNN_HEAD_DIM, RNN_HEAD_DIM)),
            full((1, D_RNN)),
            full((N_RNN_HEADS, RNN_HEAD_DIM, RNN_HEAD_DIM)),
            full((1, D_RNN)),
            full((1, D_RNN)),
        ],
        out_specs=[
            pl.BlockSpec((TM, D_RNN), lambda i: (i, 0)),
            full((nb, D_RNN)),
            full((halo, D_RNN)),
        ],
        out_shape=[
            jax.ShapeDtypeStruct((seq * nb, D_RNN), F32),
            jax.ShapeDtypeStruct((nb, D_RNN), F32),
            jax.ShapeDtypeStruct((halo, D_RNN), F32),
        ],
        scratch_shapes=[
            pltpu.VMEM((halo + TM, D_RNN), F32),
            pltpu.VMEM((TM, D_RNN), F32),
            pltpu.VMEM((TM, D_RNN), F32),
            pltpu.VMEM((TM, D_RNN), F32),
            pltpu.VMEM((nb, D_RNN), F32),
        ],
        compiler_params=_cparams(1),
        name=name,
    )(proj, conv0, h0, cw, cb.reshape(1, D_RNN), wa, ba.reshape(1, D_RNN), wi, bi.reshape(1, D_RNN),
      lam.reshape(1, D_RNN))


def _conf_kernel(u_ref, conv0_ref, dw_ref, db_ref, g_ref, b_ref, o_ref, convnew_ref, win_ref, d_ref,
                 *, nb, tl, n_tiles):
    rows = tl * nb
    halo = (CONF_CONV_W - 1) * nb
    rc = 64
    cc = 256

    @pl.when(pl.program_id(0) == 0)
    def _():
        win_ref[0:halo, :] = conv0_ref[...]

    win_ref[halo:halo + rows, :] = u_ref[...]

    for c0 in range(0, D_CONF, cc):
        cs = slice(c0, c0 + cc)

        def chunk(r, carry, cs=cs):
            r0 = pl.multiple_of(r * rc, rc)
            acc = jnp.broadcast_to(db_ref[:, cs], (rc, cc))
            for w in range(CONF_CONV_W):
                acc = acc + win_ref[pl.ds(r0 + w * nb, rc), cs] * dw_ref[w:w + 1, cs]
            d_ref[pl.ds(r0, rc), cs] = acc
            return carry

        lax.fori_loop(0, rows // rc, chunk, 0)

    convnew_ref[...] = win_ref[rows:rows + halo, :]
    if n_tiles > 1:
        win_ref[0:halo, :] = win_ref[rows:rows + halo, :]

    y = _layer_norm(d_ref[...], g_ref[...], b_ref[...])
    o_ref[...] = y * jax.nn.sigmoid(y)


def _conf_call(u, conv0, dw, db, g, b, *, nb, seq, row_block0, name):
    tl = TM // nb
    n_tiles = seq // tl
    halo = (CONF_CONV_W - 1) * nb
    full = lambda shape: pl.BlockSpec(shape, lambda i: (0,) * len(shape))
    return pl.pallas_call(
        functools.partial(_conf_kernel, nb=nb, tl=tl, n_tiles=n_tiles),
        grid=(n_tiles,),
        in_specs=[
            pl.BlockSpec((TM, D_CONF), lambda i: (row_block0 + i, 0)),
            full((halo, D_CONF)),
            full((CONF_CONV_W, D_CONF)),
            full((1, D_CONF)),
            full((1, D_CONF)),
            full((1, D_CONF)),
        ],
        out_specs=[
            pl.BlockSpec((TM, D_CONF), lambda i: (i, 0)),
            full((halo, D_CONF)),
        ],
        out_shape=[
            jax.ShapeDtypeStruct((seq * nb, D_CONF), F32),
            jax.ShapeDtypeStruct((halo, D_CONF), F32),
        ],
        scratch_shapes=[
            pltpu.VMEM((halo + TM, D_CONF), F32),
            pltpu.VMEM((TM, D_CONF), F32),
        ],
        compiler_params=_cparams(1),
        name=name,
    )(u, conv0, dw, db.reshape(1, D_CONF), g.reshape(1, D_CONF), b.reshape(1, D_CONF))


def _proj_ln_kernel(y_ref, w_ref, x_ref, gm_ref, lng_ref, lnb_ref, scf_ref, shf_ref, *rest):
    x1_ref, hf_ref = rest[-2:]
    out = jnp.dot(y_ref[...].astype(BF16), w_ref[...], preferred_element_type=F32)
    if len(rest) == 3:
        out = out + rest[0][...]
    z = DEEPNORM_ALPHA * x_ref[...] + _affine(out, gm_ref[0])
    x1 = _layer_norm(z, lng_ref[...], lnb_ref[...])
    x1_ref[...] = x1
    hf_ref[...] = _affine(x1, 1.0 + scf_ref[0], shf_ref[0])


def _proj_ln_call(y, w_bf16, bias, x, gm, lng, lnb, scf, shf, n_prompt_tiles, name):
    t, k = y.shape
    sel = lambda i: (jnp.where(i >= n_prompt_tiles, 1, 0), 0, 0)
    row = pl.BlockSpec((1, D_MODEL), lambda i: (0, 0))
    mod = pl.BlockSpec((1, GROUP, D_MODEL), sel)
    tile = pl.BlockSpec((TM, D_MODEL), lambda i: (i, 0))
    in_specs = [
        pl.BlockSpec((TM, k), lambda i: (i, 0)),
        pl.BlockSpec((k, D_MODEL), lambda i: (0, 0)),
        tile, mod, row, row, mod, mod,
    ]
    args = [y, w_bf16, x, gm, lng.reshape(1, D_MODEL), lnb.reshape(1, D_MODEL), scf, shf]
    if bias is not None:
        in_specs.append(row)
        args.append(bias.reshape(1, D_MODEL))
    return pl.pallas_call(
        _proj_ln_kernel,
        grid=(t // TM,),
        in_specs=in_specs,
        out_specs=[tile, tile],
        out_shape=[jax.ShapeDtypeStruct((t, D_MODEL), F32)] * 2,
        compiler_params=_cparams(1),
        name=name,
    )(*args)


def _top_rows(s, k):
    n = s.shape[0]
    iota = lax.broadcasted_iota(jnp.int32, s.shape, 0)
    vals, idxs = [], []
    for _ in range(k):
        m = jnp.max(s, axis=0, keepdims=True)
        am = jnp.min(jnp.where(s == m, iota, n), axis=0, keepdims=True)
        vals.append(m)
        idxs.append(am)
        s = jnp.where(iota == am, -jnp.inf, s)
    return vals, idxs


def _retrieve_kernel(hf_ref, wq_ref, keys_ref, idx_ref, g_ref):
    q = jnp.dot(hf_ref[...].astype(BF16), wq_ref[...], preferred_element_type=F32).astype(BF16)
    g_rows = []
    for h in range(N_PEER_HEADS):
        halves = []
        for p in range(2):
            hp = h * 2 + p
            s = lax.dot_general(keys_ref[hp], q[:, hp * HALF_KEY:(hp + 1) * HALF_KEY], _NT,
                                preferred_element_type=F32)
            halves.append(_top_rows(s, TOPK))
        (v1, i1), (v2, i2) = halves
        v2c = jnp.concatenate(v2, axis=0)
        i2c = jnp.concatenate(i2, axis=0)
        cand = jnp.concatenate([v1[i] + v2c for i in range(TOPK)], axis=0)
        cidx = jnp.concatenate([i1[i] * N_KEYS + i2c for i in range(TOPK)], axis=0)
        iota = lax.broadcasted_iota(jnp.int32, cand.shape, 0)
        top_s, experts = [], []
        for _ in range(TOPK):
            m = jnp.max(cand, axis=0, keepdims=True)
            am = jnp.min(jnp.where(cand == m, iota, TOPK * TOPK), axis=0, keepdims=True)
            hit = iota == am
            experts.append(jnp.max(jnp.where(hit, cidx, -1), axis=0, keepdims=True))
            top_s.append(m)
            cand = jnp.where(hit, -jnp.inf, cand)
        e = [jnp.exp(sv - top_s[0]) for sv in top_s]
        denom = e[0]
        for ev in e[1:]:
            denom = denom + ev
        g_rows.extend(ev / denom for ev in e)
        idx_ref[h * TOPK:(h + 1) * TOPK, :] = jnp.concatenate(experts, axis=0) * SLABS
    g_ref[...] = jnp.concatenate(g_rows, axis=0).T


def _retrieve_call(hf, wq_bf16, keys_bf16):
    t = hf.shape[0]
    nq = wq_bf16.shape[1]
    return pl.pallas_call(
        _retrieve_kernel,
        grid=(t // TT,),
        in_specs=[
            pl.BlockSpec((TT, D_MODEL), lambda i: (i, 0)),
            pl.BlockSpec((D_MODEL, nq), lambda i: (0, 0)),
            pl.BlockSpec((2 * N_PEER_HEADS, N_KEYS, HALF_KEY), lambda i: (0, 0, 0)),
        ],
        out_specs=[
            pl.BlockSpec((N_SEL, TT), lambda i: (0, i)),
            pl.BlockSpec((TT, N_SEL), lambda i: (i, 0)),
        ],
        out_shape=[
            jax.ShapeDtypeStruct((N_SEL, t), jnp.int32),
            jax.ShapeDtypeStruct((t, N_SEL), F32),
        ],
        compiler_params=_cparams(1),
        name="peer_retrieve",
    )(hf, wq_bf16, keys_bf16)


def _pack_table(tab):
    b = lax.bitcast_convert_type(tab.astype(BF16), jnp.uint16).astype(jnp.uint32)
    w = (b[:, :WORDS] << 16) | b[:, WORDS:]
    return w.reshape(N_EXPERTS * SLABS, LANES)


def _gather_token(idx_ref, t, tab_ref, stage_ref):
    for k in range(N_SEL):
        i = pl.multiple_of(idx_ref[k, t], SLABS)
        stage_ref[pl.ds(k, SLABS, stride=STAGE_STRIDE), :] = tab_ref[pl.ds(i, SLABS), :]


def _unpack_stage(stage_ref):
    his, los = [], []
    for j in range(SLABS):
        w = stage_ref[j * STAGE_STRIDE:j * STAGE_STRIDE + N_SEL, :]
        his.append(lax.bitcast_convert_type(w & jnp.uint32(0xFFFF0000), F32).astype(BF16))
        los.append(lax.bitcast_convert_type(w << 16, F32).astype(BF16))
    return jnp.concatenate(his + los, axis=1)


def _token_pipeline(idx_ref, tab_ref, stages, compute):
    for q in range(N_STAGE):
        _gather_token(idx_ref, q, tab_ref, stages[q])

    def body(i, carry):
        t0 = i * N_STAGE
        for q in range(N_STAGE):
            compute(t0 + q, stages[q])
            _gather_token(idx_ref, jnp.minimum(t0 + N_STAGE + q, TT - 1), tab_ref, stages[q])
        return carry

    lax.fori_loop(0, TT // N_STAGE, body, 0)


def _peer_u_kernel(idx_ref, x_ref, tab_ref, act_ref, *stages):
    def compute(t, stage):
        xr = jnp.broadcast_to(x_ref[pl.ds(t, 1), :], (8, D_MODEL)).astype(BF16)
        a = lax.dot_general(xr, _unpack_stage(stage), _NT, preferred_element_type=F32)
        act_ref[pl.ds(t, 1), :] = a[0:1]

    _token_pipeline(idx_ref, tab_ref, stages, compute)


def _peer_u_call(idx, hf, tab):
    t = hf.shape[0]
    return pl.pallas_call(
        _peer_u_kernel,
        grid=(t // TT,),
        in_specs=[
            pl.BlockSpec((N_SEL, TT), lambda i: (0, i), memory_space=pltpu.SMEM),
            pl.BlockSpec((TT, D_MODEL), lambda i: (i, 0)),
            pl.BlockSpec(memory_space=pltpu.VMEM),
        ],
        out_specs=pl.BlockSpec((TT, N_SEL), lambda i: (i, 0)),
        out_shape=jax.ShapeDtypeStruct((t, N_SEL), F32),
        scratch_shapes=[pltpu.VMEM((SLABS * STAGE_STRIDE, LANES), jnp.uint32)] * N_STAGE,
        compiler_params=_cparams(1),
        name="peer_u",
    )(idx, hf, tab)


def _peer_v_kernel(idx_ref, act_ref, g_ref, tab_ref, x_ref, gf_ref, lng_ref, lnb_ref, o_ref, w_ref, f_ref, *stages):
    w_ref[...] = g_ref[...] * _gelu(act_ref[...])

    def compute(t, stage):
        wr = jnp.broadcast_to(w_ref[pl.ds(t, 1), :], (8, N_SEL)).astype(BF16)
        f = jnp.dot(wr, _unpack_stage(stage), preferred_element_type=F32)
        f_ref[pl.ds(t, 1), :] = f[0:1]

    _token_pipeline(idx_ref, tab_ref, stages, compute)
    z = DEEPNORM_ALPHA * x_ref[...] + gf_ref[0] * f_ref[...]
    o_ref[...] = _layer_norm(z, lng_ref[...], lnb_ref[...])


def _peer_v_call(idx, act, g, tab, x, gf, lng, lnb, n_prompt_tiles):
    t = x.shape[0]
    sel = lambda i: (jnp.where(i >= n_prompt_tiles, 1, 0), 0, 0)
    row = pl.BlockSpec((1, D_MODEL), lambda i: (0, 0))
    return pl.pallas_call(
        _peer_v_kernel,
        grid=(t // TT,),
        in_specs=[
            pl.BlockSpec((N_SEL, TT), lambda i: (0, i), memory_space=pltpu.SMEM),
            pl.BlockSpec((TT, N_SEL), lambda i: (i, 0)),
            pl.BlockSpec((TT, N_SEL), lambda i: (i, 0)),
            pl.BlockSpec(memory_space=pltpu.VMEM),
            pl.BlockSpec((TT, D_MODEL), lambda i: (i, 0)),
            pl.BlockSpec((1, GROUP, D_MODEL), sel),
            row, row,
        ],
        out_specs=pl.BlockSpec((TT, D_MODEL), lambda i: (i, 0)),
        out_shape=jax.ShapeDtypeStruct((t, D_MODEL), F32),
        scratch_shapes=[pltpu.VMEM((TT, N_SEL), F32), pltpu.VMEM((TT, D_MODEL), F32)]
        + [pltpu.VMEM((SLABS * STAGE_STRIDE, LANES), jnp.uint32)] * N_STAGE,
        compiler_params=_cparams(1),
        name="peer_v",
    )(idx, act, g, tab, x, gf, lng.reshape(1, D_MODEL), lnb.reshape(1, D_MODEL))


def _time_major(a):
    return jnp.swapaxes(a, 0, 1).reshape(a.shape[0] * a.shape[1], a.shape[2])


def _batch_major(a, nb):
    return jnp.swapaxes(a.reshape(a.shape[0] // nb, nb, a.shape[1]), 0, 1)


def kernel(x_prompt, x_sample, state_lru_h, state_lru_conv, state_conf_conv, c_prompt, c_sample, w_ada, b_ada, ln_mix_g, ln_mix_b, ln_ffn_g, ln_ffn_b, lru_w_in, lru_conv_w, lru_conv_b, lru_w_a, lru_b_a, lru_w_i, lru_b_i, lru_lambda, lru_w_out, conf_w_pw1, conf_b_pw1, conf_dw_w, conf_dw_b, conf_ln_g, conf_ln_b, conf_w_pw2, conf_b_pw2, peer_w_q, peer_keys, peer_u, peer_v):
    bp, lp, _ = x_prompt.shape
    bs, ls, _ = x_sample.shape
    tp, ts = bp * lp, bs * ls
    assert GROUP % bp == 0 and bs == GROUP and TM % GROUP == 0
    for nb, seq in ((bp, lp), (bs, ls)):
        assert (seq * nb) % TM == 0 and (seq * nb == TM or TM // nb >= CONF_CONV_W - 1)
    npt_m, npt_t = tp // TM, tp // TT

    x = jnp.concatenate([_time_major(x_prompt), _time_major(x_sample)], axis=0)

    mod = _mod_call(jnp.concatenate([c_prompt, c_sample], axis=0), w_ada, b_ada)
    mod = mod.reshape(DEPTH, bp + bs, 6, D_MODEL)
    pat = jnp.stack([jnp.tile(mod[:, :bp], (1, GROUP // bp, 1, 1)), mod[:, bp:]], axis=1)
    pat = jnp.transpose(pat, (0, 3, 1, 2, 4))
    mods = [[pat[i, k] for k in range(6)] for i in range(DEPTH)]

    zeros = lambda *s: jnp.zeros(s, F32)
    new_h, new_lconv, new_cconv = [], [], []
    for i in range(DEPTH):
        sh_m, sc_m, g_m, sh_f, sc_f, g_f = mods[i]
        j = i // 2
        if i % 2 == 0:
            proj = _mm_mod_call(x, sc_m, sh_m, lru_w_in[j].astype(BF16), None, npt_m, False, "lru_in_proj")
            wa, wi = lru_w_a[j].astype(BF16), lru_w_i[j].astype(BF16)
            common = (lru_conv_w[j], lru_conv_b[j], wa, lru_b_a[j], wi, lru_b_i[j], lru_lambda[j])
            yp, hp, cp = _lru_call(proj, zeros((LRU_CONV_W - 1) * bp, D_RNN), zeros(bp, D_RNN), *common,
                                   nb=bp, seq=lp, row_block0=0, name="lru_seq_prompt")
            ys, hs, cs = _lru_call(proj, _time_major(state_lru_conv[j]), state_lru_h[j], *common,
                                   nb=bs, seq=ls, row_block0=npt_m, name="lru_seq_sample")
            new_h.append((hp, hs))
            new_lconv.append((_batch_major(cp, bp), _batch_major(cs, bs)))
            mixed, w_out, b_out = jnp.concatenate([yp, ys], axis=0), lru_w_out[j], None
        else:
            u = _mm_mod_call(x, sc_m, sh_m, conf_w_pw1[j].astype(BF16), conf_b_pw1[j], npt_m, True, "conf_pw1_glu")
            common = (conf_dw_w[j], conf_dw_b[j], conf_ln_g[j], conf_ln_b[j])
            dp, cp = _conf_call(u, zeros((CONF_CONV_W - 1) * bp, D_CONF), *common,
                                nb=bp, seq=lp, row_block0=0, name="conf_seq_prompt")
            ds, cs = _conf_call(u, _time_major(state_conf_conv[j]), *common,
                                nb=bs, seq=ls, row_block0=npt_m, name="conf_seq_sample")
            new_cconv.append((_batch_major(cp, bp), _batch_major(cs, bs)))
            mixed, w_out, b_out = jnp.concatenate([dp, ds], axis=0), conf_w_pw2[j], conf_b_pw2[j]

        x, hf = _proj_ln_call(mixed, w_out.astype(BF16), b_out, x, g_m, ln_mix_g[i], ln_mix_b[i], sc_f, sh_f,
                              npt_m, "mix_out_ln")
        idx, gate = _retrieve_call(hf, peer_w_q[i].astype(BF16),
                                   peer_keys[i].astype(BF16).reshape(2 * N_PEER_HEADS, N_KEYS, HALF_KEY))
        act = _peer_u_call(idx, hf, _pack_table(peer_u[i]))
        x = _peer_v_call(idx, act, gate, _pack_table(peer_v[i]), x, g_f, ln_ffn_g[i], ln_ffn_b[i], npt_t)

    y_prompt = _batch_major(x[:tp], bp)
    y_sample = _batch_major(x[tp:], bs)
    pick = lambda pairs, k: jnp.stack([p[k] for p in pairs])
    return (y_prompt, y_sample, pick(new_h, 0), pick(new_lconv, 0), pick(new_cconv, 0),
            pick(new_h, 1), pick(new_lconv, 1), pick(new_cconv, 1))
```

```python
import functools
import math

import jax
import jax.numpy as jnp
from jax import lax
from jax.experimental import pallas as pl
from jax.experimental.pallas import tpu as pltpu

F32 = jnp.float32
BF16 = jnp.bfloat16

D_MODEL = 1024
DEPTH = 2
D_RNN = 1280
N_RNN_HEADS = 10
RNN_HEAD_DIM = 128
LRU_CONV_W = 4
LRU_C = 8.0
D_CONF = 1024
CONF_CONV_W = 31
N_PEER_HEADS = 8
N_KEYS = 128
N_EXPERTS = N_KEYS * N_KEYS
HALF_KEY = 128
TOPK = 16
N_SEL = N_PEER_HEADS * TOPK
DEEPNORM_ALPHA = (2 * DEPTH) ** 0.25
LN_EPS = 1e-5

LANES = 128
GROUP = 128
TM = 512
TT = 128
WORDS = D_MODEL // 2
SLABS = WORDS // LANES
STAGE_STRIDE = 136
N_STAGE = 4
VMEM_LIMIT = 56 * 1024 * 1024

_NT = (((1,), (1,)), ((), ()))


def _cparams(n_axes, vmem=VMEM_LIMIT):
    return pltpu.CompilerParams(dimension_semantics=("arbitrary",) * n_axes, vmem_limit_bytes=vmem)


def _gelu(x):
    return 0.5 * x * (1.0 + lax.erf(x * (1.0 / math.sqrt(2.0))))


def _affine(x, scale, shift=None):
    rows, c = x.shape
    xr = x.reshape(rows // GROUP, GROUP, c) * scale[None]
    if shift is not None:
        xr = xr + shift[None]
    return xr.reshape(rows, c)


def _layer_norm(z, g, b):
    mu = jnp.mean(z, axis=-1, keepdims=True)
    zc = z - mu
    var = jnp.mean(zc * zc, axis=-1, keepdims=True)
    return zc * lax.rsqrt(var + LN_EPS) * g + b


def _mod_kernel(c_ref, w_ref, b_ref, o_ref):
    c = c_ref[...]
    a = (c * jax.nn.sigmoid(c)).astype(BF16)
    o_ref[0] = jnp.dot(a, w_ref[0].astype(BF16), preferred_element_type=F32) + b_ref[0]


def _mod_call(c_all, w_ada, b_ada):
    nb = c_all.shape[0]
    tn = 1536
    return pl.pallas_call(
        _mod_kernel,
        grid=(DEPTH, 6 * D_MODEL // tn),
        in_specs=[
            pl.BlockSpec((nb, D_MODEL), lambda l, j: (0, 0)),
            pl.BlockSpec((1, D_MODEL, tn), lambda l, j: (l, 0, j)),
            pl.BlockSpec((1, 1, tn), lambda l, j: (l, 0, j)),
        ],
        out_specs=pl.BlockSpec((1, nb, tn), lambda l, j: (l, 0, j)),
        out_shape=jax.ShapeDtypeStruct((DEPTH, nb, 6 * D_MODEL), F32),
        compiler_params=_cparams(2),
        name="adaln_mod",
    )(c_all, w_ada, b_ada.reshape(DEPTH, 1, 6 * D_MODEL))


def _mm_mod_kernel(x_ref, sc_ref, sh_ref, w_ref, *rest, glu):
    o_ref = rest[-1]
    xm = _affine(x_ref[...], 1.0 + sc_ref[0], sh_ref[0])
    p = jnp.dot(xm.astype(BF16), w_ref[...], preferred_element_type=F32)
    if len(rest) == 2:
        p = p + rest[0][...]
    if glu:
        n = p.shape[-1] // 2
        p = p[:, :n] * jax.nn.sigmoid(p[:, n:])
    o_ref[...] = p


def _mm_mod_call(x, sc, sh, w_bf16, bias, n_prompt_tiles, glu, name):
    t = x.shape[0]
    n = w_bf16.shape[1]
    n_out = n // 2 if glu else n
    sel = lambda i: (jnp.where(i >= n_prompt_tiles, 1, 0), 0, 0)
    in_specs = [
        pl.BlockSpec((TM, D_MODEL), lambda i: (i, 0)),
        pl.BlockSpec((1, GROUP, D_MODEL), sel),
        pl.BlockSpec((1, GROUP, D_MODEL), sel),
        pl.BlockSpec((D_MODEL, n), lambda i: (0, 0)),
    ]
    args = [x, sc, sh, w_bf16]
    if bias is not None:
        in_specs.append(pl.BlockSpec((1, n), lambda i: (0, 0)))
        args.append(bias.reshape(1, n))
    return pl.pallas_call(
        functools.partial(_mm_mod_kernel, glu=glu),
        grid=(t // TM,),
        in_specs=in_specs,
        out_specs=pl.BlockSpec((TM, n_out), lambda i: (i, 0)),
        out_shape=jax.ShapeDtypeStruct((t, n_out), F32),
        compiler_params=_cparams(1),
        name=name,
    )(*args)


def _lru_kernel(proj_ref, conv0_ref, h0_ref, cw_ref, cb_ref, wa_ref, ba_ref, wi_ref, bi_ref, lam_ref,
                yg_ref, hnew_ref, convnew_ref, win_ref, xc_ref, a_ref, b_ref, h_ref, *, nb, tl, cw, n_tiles):
    rows = tl * nb
    halo = (LRU_CONV_W - 1) * nb

    @pl.when(pl.program_id(0) == 0)
    def _():
        win_ref[0:halo, :] = conv0_ref[...]
        h_ref[...] = h0_ref[...]

    win_ref[halo:halo + rows, :] = proj_ref[:, :D_RNN]
    xc = cb_ref[...] + win_ref[0:rows, :] * cw_ref[0:1, :]
    for w in range(1, LRU_CONV_W):
        xc = xc + win_ref[w * nb:w * nb + rows, :] * cw_ref[w:w + 1, :]
    xc_ref[...] = xc
    carry = win_ref[rows:rows + halo, :]
    convnew_ref[...] = carry
    if n_tiles > 1:
        win_ref[0:halo, :] = carry

    lam = lam_ref[...]
    neg = -lam
    softplus = jnp.maximum(neg, 0.0) + jnp.log1p(jnp.exp(-jnp.abs(neg)))
    for hd in range(N_RNN_HEADS):
        cs = slice(hd * RNN_HEAD_DIM, (hd + 1) * RNN_HEAD_DIM)
        xh = xc_ref[:, cs]
        xb = xh.astype(BF16)
        r = jax.nn.sigmoid(jnp.dot(xb, wa_ref[hd], preferred_element_type=F32) + ba_ref[:, cs])
        ig = jax.nn.sigmoid(jnp.dot(xb, wi_ref[hd], preferred_element_type=F32) + bi_ref[:, cs])
        log_a = (-LRU_C * r) * softplus[:, cs]
        th = jnp.tanh(log_a)
        a_ref[:, cs] = jnp.exp(log_a)
        b_ref[:, cs] = jnp.sqrt(-2.0 * th / (1.0 - th)) * (ig * xh)

    for c0 in range(0, D_RNN, cw):
        cs = slice(c0, c0 + cw)

        def step(l, h, cs=cs):
            rs = pl.ds(pl.multiple_of(l * nb, nb), nb)
            h = a_ref[rs, cs] * h + b_ref[rs, cs]
            b_ref[rs, cs] = h
            return h

        h_ref[:, cs] = lax.fori_loop(0, tl, step, h_ref[:, cs])

    yg_ref[...] = b_ref[...] * _gelu(proj_ref[:, D_RNN:])
    hnew_ref[...] = h_ref[...]


def _lru_call(proj, conv0, h0, cw, cb, wa, ba, wi, bi, lam, *, nb, seq, row_block0, name):
    tl = TM // nb
    n_tiles = seq // tl
    halo = (LRU_CONV_W - 1) * nb
    chan = D_RNN if nb * D_RNN <= 16 * 1024 else LANES
    full = lambda shape: pl.BlockSpec(shape, lambda i: (0,) * len(shape))
    return pl.pallas_call(
        functools.partial(_lru_kernel, nb=nb, tl=tl, cw=chan, n_tiles=n_tiles),
        grid=(n_tiles,),
        in_specs=[
            pl.BlockSpec((TM, 2 * D_RNN), lambda i: (row_block0 + i, 0)),
            full((halo, D_RNN)),
            full((nb, D_RNN)),
            full((LRU_CONV_W, D_RNN)),
            full((1, D_RNN)),
            full((N_RNN_HEADS, RNN_HEAD_DIM, RNN_HEAD_DIM)),
            full((1, D_RNN)),
            full((N_RNN_HEADS, RNN_HEAD_DIM, RNN_HEAD_DIM)),
            full((1, D_RNN)),
            full((1, D_RNN)),
        ],
        out_specs=[
            pl.BlockSpec((TM, D_RNN), lambda i: (i, 0)),
            full((nb, D_RNN)),
            full((halo, D_RNN)),
        ],
        out_shape=[
            jax.ShapeDtypeStruct((seq * nb, D_RNN), F32),
            jax.ShapeDtypeStruct((nb, D_RNN), F32),
            jax.ShapeDtypeStruct((halo, D_RNN), F32),
        ],
        scratch_shapes=[
            pltpu.VMEM((halo + TM, D_RNN), F32),
            pltpu.VMEM((TM, D_RNN), F32),
            pltpu.VMEM((TM, D_RNN), F32),
            pltpu.VMEM((TM, D_RNN), F32),
            pltpu.VMEM((nb, D_RNN), F32),
        ],
        compiler_params=_cparams(1),
        name=name,
    )(proj, conv0, h0, cw, cb.reshape(1, D_RNN), wa, ba.reshape(1, D_RNN), wi, bi.reshape(1, D_RNN),
      lam.reshape(1, D_RNN))


def _conf_kernel(u_ref, conv0_ref, dw_ref, db_ref, g_ref, b_ref, o_ref, convnew_ref, win_ref, d_ref,
                 *, nb, tl, n_tiles):
    rows = tl * nb
    halo = (CONF_CONV_W - 1) * nb
    rc = 64
    cc = 256

    @pl.when(pl.program_id(0) == 0)
    def _():
        win_ref[0:halo, :] = conv0_ref[...]

    win_ref[halo:halo + rows, :] = u_ref[...]

    for c0 in range(0, D_CONF, cc):
        cs = slice(c0, c0 + cc)

        def chunk(r, carry, cs=cs):
            r0 = pl.multiple_of(r * rc, rc)
            acc = jnp.broadcast_to(db_ref[:, cs], (rc, cc))
            for w in range(CONF_CONV_W):
                acc = acc + win_ref[pl.ds(r0 + w * nb, rc), cs] * dw_ref[w:w + 1, cs]
            d_ref[pl.ds(r0, rc), cs] = acc
            return carry

        lax.fori_loop(0, rows // rc, chunk, 0)

    convnew_ref[...] = win_ref[rows:rows + halo, :]
    if n_tiles > 1:
        win_ref[0:halo, :] = win_ref[rows:rows + halo, :]

    y = _layer_norm(d_ref[...], g_ref[...], b_ref[...])
    o_ref[...] = y * jax.nn.sigmoid(y)


def _conf_call(u, conv0, dw, db, g, b, *, nb, seq, row_block0, name):
    tl = TM // nb
    n_tiles = seq // tl
    halo = (CONF_CONV_W - 1) * nb
    full = lambda shape: pl.BlockSpec(shape, lambda i: (0,) * len(shape))
    return pl.pallas_call(
        functools.partial(_conf_kernel, nb=nb, tl=tl, n_tiles=n_tiles),
        grid=(n_tiles,),
        in_specs=[
            pl.BlockSpec((TM, D_CONF), lambda i: (row_block0 + i, 0)),
            full((halo, D_CONF)),
            full((CONF_CONV_W, D_CONF)),
            full((1, D_CONF)),
            full((1, D_CONF)),
            full((1, D_CONF)),
        ],
        out_specs=[
            pl.BlockSpec((TM, D_CONF), lambda i: (i, 0)),
            full((halo, D_CONF)),
        ],
        out_shape=[
            jax.ShapeDtypeStruct((seq * nb, D_CONF), F32),
            jax.ShapeDtypeStruct((halo, D_CONF), F32),
        ],
        scratch_shapes=[
            pltpu.VMEM((halo + TM, D_CONF), F32),
            pltpu.VMEM((TM, D_CONF), F32),
        ],
        compiler_params=_cparams(1),
        name=name,
    )(u, conv0, dw, db.reshape(1, D_CONF), g.reshape(1, D_CONF), b.reshape(1, D_CONF))


def _proj_ln_kernel(y_ref, w_ref, x_ref, gm_ref, lng_ref, lnb_ref, scf_ref, shf_ref, *rest):
    x1_ref, hf_ref = rest[-2:]
    out = jnp.dot(y_ref[...].astype(BF16), w_ref[...], preferred_element_type=F32)
    if len(rest) == 3:
        out = out + rest[0][...]
    z = DEEPNORM_ALPHA * x_ref[...] + _affine(out, gm_ref[0])
    x1 = _layer_norm(z, lng_ref[...], lnb_ref[...])
    x1_ref[...] = x1
    hf_ref[...] = _affine(x1, 1.0 + scf_ref[0], shf_ref[0])


def _proj_ln_call(y, w_bf16, bias, x, gm, lng, lnb, scf, shf, n_prompt_tiles, name):
    t, k = y.shape
    sel = lambda i: (jnp.where(i >= n_prompt_tiles, 1, 0), 0, 0)
    row = pl.BlockSpec((1, D_MODEL), lambda i: (0, 0))
    mod = pl.BlockSpec((1, GROUP, D_MODEL), sel)
    tile = pl.BlockSpec((TM, D_MODEL), lambda i: (i, 0))
    in_specs = [
        pl.BlockSpec((TM, k), lambda i: (i, 0)),
        pl.BlockSpec((k, D_MODEL), lambda i: (0, 0)),
        tile, mod, row, row, mod, mod,
    ]
    args = [y, w_bf16, x, gm, lng.reshape(1, D_MODEL), lnb.reshape(1, D_MODEL), scf, shf]
    if bias is not None:
        in_specs.append(row)
        args.append(bias.reshape(1, D_MODEL))
    return pl.pallas_call(
        _proj_ln_kernel,
        grid=(t // TM,),
        in_specs=in_specs,
        out_specs=[tile, tile],
        out_shape=[jax.ShapeDtypeStruct((t, D_MODEL), F32)] * 2,
        compiler_params=_cparams(1),
        name=name,
    )(*args)


SUB = 8

_CELLS = [(i, j) for i in range(TOPK) for j in range(TOPK) if (i + 1) * (j + 1) <= TOPK]
_CELLS = _CELLS + [None] * (-len(_CELLS) % SUB)
_CELL_GROUPS = [_CELLS[k:k + SUB] for k in range(0, len(_CELLS), SUB)]


def _stack_rows(rows, r_iota):
    out = jnp.broadcast_to(rows[-1], r_iota.shape)
    for r in range(SUB - 2, -1, -1):
        if rows[r] is not rows[r + 1]:
            out = jnp.where(r_iota <= r, rows[r], out)
    return out


def _top_rows(s, k):
    n = s.shape[0]
    iota = lax.broadcasted_iota(jnp.int32, s.shape, 0)
    vals, idxs = [], []
    for _ in range(k):
        m = jnp.max(s, axis=0, keepdims=True)
        am = jnp.min(jnp.where(s == m, iota, n), axis=0, keepdims=True)
        vals.append(m)
        idxs.append(am)
        s = jnp.where(iota == am, -jnp.inf, s)
    return vals, idxs


def _retrieve_kernel(hf_ref, wq_ref, keys_ref, idx_ref, g_ref, idx_t_ref, g_t_ref):
    q = jnp.dot(hf_ref[...].astype(BF16), wq_ref[...], preferred_element_type=F32).astype(BF16)
    r_iota = lax.broadcasted_iota(jnp.int32, (SUB, TT), 0)
    neg_inf = jnp.full((1, TT), -jnp.inf, F32)
    zero = jnp.zeros((1, TT), jnp.int32)
    for h in range(N_PEER_HEADS):
        halves = []
        for p in range(2):
            hp = h * 2 + p
            s = lax.dot_general(keys_ref[hp], q[:, hp * HALF_KEY:(hp + 1) * HALF_KEY], _NT,
                                preferred_element_type=F32)
            halves.append(_top_rows(s, TOPK))
        (v1, i1), (v2, i2) = halves
        i1 = [iv * N_KEYS for iv in i1]
        cand, cidx = [], []
        for cells in _CELL_GROUPS:
            pick = lambda rows, pos, pad: _stack_rows([pad if c is None else rows[c[pos]] for c in cells], r_iota)
            cand.append(pick(v1, 0, neg_inf) + pick(v2, 1, neg_inf))
            cidx.append(pick(i1, 0, zero) + pick(i2, 1, zero))
        cand = jnp.concatenate(cand, axis=0)
        cidx = jnp.concatenate(cidx, axis=0)
        iota = lax.broadcasted_iota(jnp.int32, cand.shape, 0)
        top_s, experts = [], []
        for _ in range(TOPK):
            m = jnp.max(cand, axis=0, keepdims=True)
            am = jnp.min(jnp.where(cand == m, iota, len(_CELLS)), axis=0, keepdims=True)
            hit = iota == am
            experts.append(jnp.max(jnp.where(hit, cidx, -1), axis=0, keepdims=True))
            top_s.append(m)
            cand = jnp.where(hit, -jnp.inf, cand)
        e = [jnp.exp(_stack_rows(top_s[k:k + SUB], r_iota) - top_s[0]) for k in range(0, TOPK, SUB)]
        denom = jnp.sum(sum(e[1:], e[0]), axis=0, keepdims=True)
        for k, ev in enumerate(e):
            rows = slice(h * TOPK + k * SUB, h * TOPK + (k + 1) * SUB)
            g_t_ref[rows, :] = ev / denom
            idx_t_ref[rows, :] = _stack_rows(experts[k * SUB:(k + 1) * SUB], r_iota) * SLABS
    g_ref[...] = g_t_ref[...].T
    idx_ref[...] = idx_t_ref[...].T


def _retrieve_call(hf, wq_bf16, keys_bf16):
    t = hf.shape[0]
    nq = wq_bf16.shape[1]
    tile = pl.BlockSpec((TT, N_SEL), lambda i: (i, 0))
    return pl.pallas_call(
        _retrieve_kernel,
        grid=(t // TT,),
        in_specs=[
            pl.BlockSpec((TT, D_MODEL), lambda i: (i, 0)),
            pl.BlockSpec((D_MODEL, nq), lambda i: (0, 0)),
            pl.BlockSpec((2 * N_PEER_HEADS, N_KEYS, HALF_KEY), lambda i: (0, 0, 0)),
        ],
        out_specs=[tile, tile],
        out_shape=[
            jax.ShapeDtypeStruct((t, N_SEL), jnp.int32),
            jax.ShapeDtypeStruct((t, N_SEL), F32),
        ],
        scratch_shapes=[pltpu.VMEM((N_SEL, TT), jnp.int32), pltpu.VMEM((N_SEL, TT), F32)],
        compiler_params=_cparams(1),
        name="peer_retrieve",
    )(hf, wq_bf16, keys_bf16)


def _pack_table(tab):
    b = lax.bitcast_convert_type(tab.astype(BF16), jnp.uint16).astype(jnp.uint32)
    w = (b[:, :WORDS] << 16) | b[:, WORDS:]
    return w.reshape(N_EXPERTS * SLABS, LANES)


def _gather_token(idx_ref, t, tab_ref, stage_ref):
    for k in range(N_SEL):
        i = pl.multiple_of(idx_ref[t, k], SLABS)
        stage_ref[pl.ds(k, SLABS, stride=STAGE_STRIDE), :] = tab_ref[pl.ds(i, SLABS), :]


def _unpack_stage(stage_ref):
    his, los = [], []
    for j in range(SLABS):
        w = stage_ref[j * STAGE_STRIDE:j * STAGE_STRIDE + N_SEL, :]
        his.append(lax.bitcast_convert_type(w & jnp.uint32(0xFFFF0000), F32).astype(BF16))
        los.append(lax.bitcast_convert_type(w << 16, F32).astype(BF16))
    return jnp.concatenate(his + los, axis=1)


def _token_pipeline(idx_ref, tab_ref, stages, compute):
    for q in range(N_STAGE):
        _gather_token(idx_ref, q, tab_ref, stages[q])

    def body(i, carry):
        t0 = i * N_STAGE
        for q in range(N_STAGE):
            compute(t0 + q, stages[q])
            _gather_token(idx_ref, jnp.minimum(t0 + N_STAGE + q, TT - 1), tab_ref, stages[q])
        return carry

    lax.fori_loop(0, TT // N_STAGE, body, 0)


def _peer_u_kernel(idx_ref, x_ref, tab_ref, act_ref, *stages):
    def compute(t, stage):
        xr = jnp.broadcast_to(x_ref[pl.ds(t, 1), :], (8, D_MODEL)).astype(BF16)
        a = lax.dot_general(xr, _unpack_stage(stage), _NT, preferred_element_type=F32)
        act_ref[pl.ds(t, 1), :] = a[0:1]

    _token_pipeline(idx_ref, tab_ref, stages, compute)


def _peer_u_call(idx, hf, tab):
    t = hf.shape[0]
    return pl.pallas_call(
        _peer_u_kernel,
        grid=(t // TT,),
        in_specs=[
            pl.BlockSpec((TT, N_SEL), lambda i: (i, 0), memory_space=pltpu.SMEM),
            pl.BlockSpec((TT, D_MODEL), lambda i: (i, 0)),
            pl.BlockSpec(memory_space=pltpu.VMEM),
        ],
        out_specs=pl.BlockSpec((TT, N_SEL), lambda i: (i, 0)),
        out_shape=jax.ShapeDtypeStruct((t, N_SEL), F32),
        scratch_shapes=[pltpu.VMEM((SLABS * STAGE_STRIDE, LANES), jnp.uint32)] * N_STAGE,
        compiler_params=_cparams(1),
        name="peer_u",
    )(idx, hf, tab)


def _peer_v_kernel(idx_ref, act_ref, g_ref, tab_ref, x_ref, gf_ref, lng_ref, lnb_ref, o_ref, w_ref, f_ref, *stages):
    w_ref[...] = g_ref[...] * _gelu(act_ref[...])

    def compute(t, stage):
        wr = jnp.broadcast_to(w_ref[pl.ds(t, 1), :], (8, N_SEL)).astype(BF16)
        f = jnp.dot(wr, _unpack_stage(stage), preferred_element_type=F32)
        f_ref[pl.ds(t, 1), :] = f[0:1]

    _token_pipeline(idx_ref, tab_ref, stages, compute)
    z = DEEPNORM_ALPHA * x_ref[...] + gf_ref[0] * f_ref[...]
    o_ref[...] = _layer_norm(z, lng_ref[...], lnb_ref[...])


def _peer_v_call(idx, act, g, tab, x, gf, lng, lnb, n_prompt_tiles):
    t = x.shape[0]
    sel = lambda i: (jnp.where(i >= n_prompt_tiles, 1, 0), 0, 0)
    row = pl.BlockSpec((1, D_MODEL), lambda i: (0, 0))
    return pl.pallas_call(
        _peer_v_kernel,
        grid=(t // TT,),
        in_specs=[
            pl.BlockSpec((TT, N_SEL), lambda i: (i, 0), memory_space=pltpu.SMEM),
            pl.BlockSpec((TT, N_SEL), lambda i: (i, 0)),
            pl.BlockSpec((TT, N_SEL), lambda i: (i, 0)),
            pl.BlockSpec(memory_space=pltpu.VMEM),
            pl.BlockSpec((TT, D_MODEL), lambda i: (i, 0)),
            pl.BlockSpec((1, GROUP, D_MODEL), sel),
            row, row,
        ],
        out_specs=pl.BlockSpec((TT, D_MODEL), lambda i: (i, 0)),
        out_shape=jax.ShapeDtypeStruct((t, D_MODEL), F32),
        scratch_shapes=[pltpu.VMEM((TT, N_SEL), F32), pltpu.VMEM((TT, D_MODEL), F32)]
        + [pltpu.VMEM((SLABS * STAGE_STRIDE, LANES), jnp.uint32)] * N_STAGE,
        compiler_params=_cparams(1),
        name="peer_v",
    )(idx, act, g, tab, x, gf, lng.reshape(1, D_MODEL), lnb.reshape(1, D_MODEL))


def _time_major(a):
    return jnp.swapaxes(a, 0, 1).reshape(a.shape[0] * a.shape[1], a.shape[2])


def _batch_major(a, nb):
    return jnp.swapaxes(a.reshape(a.shape[0] // nb, nb, a.shape[1]), 0, 1)


def kernel(x_prompt, x_sample, state_lru_h, state_lru_conv, state_conf_conv, c_prompt, c_sample, w_ada, b_ada, ln_mix_g, ln_mix_b, ln_ffn_g, ln_ffn_b, lru_w_in, lru_conv_w, lru_conv_b, lru_w_a, lru_b_a, lru_w_i, lru_b_i, lru_lambda, lru_w_out, conf_w_pw1, conf_b_pw1, conf_dw_w, conf_dw_b, conf_ln_g, conf_ln_b, conf_w_pw2, conf_b_pw2, peer_w_q, peer_keys, peer_u, peer_v):
    bp, lp, _ = x_prompt.shape
    bs, ls, _ = x_sample.shape
    tp, ts = bp * lp, bs * ls
    assert GROUP % bp == 0 and bs == GROUP and TM % GROUP == 0
    for nb, seq in ((bp, lp), (bs, ls)):
        assert (seq * nb) % TM == 0 and (seq * nb == TM or TM // nb >= CONF_CONV_W - 1)
    npt_m, npt_t = tp // TM, tp // TT

    x = jnp.concatenate([_time_major(x_prompt), _time_major(x_sample)], axis=0)

    mod = _mod_call(jnp.concatenate([c_prompt, c_sample], axis=0), w_ada, b_ada)
    mod = mod.reshape(DEPTH, bp + bs, 6, D_MODEL)
    pat = jnp.stack([jnp.tile(mod[:, :bp], (1, GROUP // bp, 1, 1)), mod[:, bp:]], axis=1)
    pat = jnp.transpose(pat, (0, 3, 1, 2, 4))
    mods = [[pat[i, k] for k in range(6)] for i in range(DEPTH)]

    zeros = lambda *s: jnp.zeros(s, F32)
    new_h, new_lconv, new_cconv = [], [], []
    for i in range(DEPTH):
        sh_m, sc_m, g_m, sh_f, sc_f, g_f = mods[i]
        j = i // 2
        if i % 2 == 0:
            proj = _mm_mod_call(x, sc_m, sh_m, lru_w_in[j].astype(BF16), None, npt_m, False, "lru_in_proj")
            wa, wi = lru_w_a[j].astype(BF16), lru_w_i[j].astype(BF16)
            common = (lru_conv_w[j], lru_conv_b[j], wa, lru_b_a[j], wi, lru_b_i[j], lru_lambda[j])
            yp, hp, cp = _lru_call(proj, zeros((LRU_CONV_W - 1) * bp, D_RNN), zeros(bp, D_RNN), *common,
                                   nb=bp, seq=lp, row_block0=0, name="lru_seq_prompt")
            ys, hs, cs = _lru_call(proj, _time_major(state_lru_conv[j]), state_lru_h[j], *common,
                                   nb=bs, seq=ls, row_block0=npt_m, name="lru_seq_sample")
            new_h.append((hp, hs))
            new_lconv.append((_batch_major(cp, bp), _batch_major(cs, bs)))
            mixed, w_out, b_out = jnp.concatenate([yp, ys], axis=0), lru_w_out[j], None
        else:
            u = _mm_mod_call(x, sc_m, sh_m, conf_w_pw1[j].astype(BF16), conf_b_pw1[j], npt_m, True, "conf_pw1_glu")
            common = (conf_dw_w[j], conf_dw_b[j], conf_ln_g[j], conf_ln_b[j])
            dp, cp = _conf_call(u, zeros((CONF_CONV_W - 1) * bp, D_CONF), *common,
                                nb=bp, seq=lp, row_block0=0, name="conf_seq_prompt")
            ds, cs = _conf_call(u, _time_major(state_conf_conv[j]), *common,
                                nb=bs, seq=ls, row_block0=npt_m, name="conf_seq_sample")
            new_cconv.append((_batch_major(cp, bp), _batch_major(cs, bs)))
            mixed, w_out, b_out = jnp.concatenate([dp, ds], axis=0), conf_w_pw2[j], conf_b_pw2[j]

        x, hf = _proj_ln_call(mixed, w_out.astype(BF16), b_out, x, g_m, ln_mix_g[i], ln_mix_b[i], sc_f, sh_f,
                              npt_m, "mix_out_ln")
        idx, gate = _retrieve_call(hf, peer_w_q[i].astype(BF16),
                                   peer_keys[i].astype(BF16).reshape(2 * N_PEER_HEADS, N_KEYS, HALF_KEY))
        act = _peer_u_call(idx, hf, _pack_table(peer_u[i]))
        x = _peer_v_call(idx, act, gate, _pack_table(peer_v[i]), x, g_f, ln_ffn_g[i], ln_ffn_b[i], npt_t)

    y_prompt = _batch_major(x[:tp], bp)
    y_sample = _batch_major(x[tp:], bs)
    pick = lambda pairs, k: jnp.stack([p[k] for p in pairs])
    return (y_prompt, y_sample, pick(new_h, 0), pick(new_lconv, 0), pick(new_cconv, 0),
            pick(new_h, 1), pick(new_lconv, 1), pick(new_cconv, 1))
```

```python
import functools
import math

import jax
import jax.numpy as jnp
from jax import lax
from jax.experimental import pallas as pl
from jax.experimental.pallas import tpu as pltpu

F32 = jnp.float32
BF16 = jnp.bfloat16

D_MODEL = 1024
DEPTH = 2
D_RNN = 1280
N_RNN_HEADS = 10
RNN_HEAD_DIM = 128
LRU_CONV_W = 4
LRU_C = 8.0
D_CONF = 1024
CONF_CONV_W = 31
N_PEER_HEADS = 8
N_KEYS = 128
N_EXPERTS = N_KEYS * N_KEYS
HALF_KEY = 128
TOPK = 16
N_SEL = N_PEER_HEADS * TOPK
DEEPNORM_ALPHA = (2 * DEPTH) ** 0.25
LN_EPS = 1e-5

LANES = 128
GROUP = 128
TM = 512
TT = 128
WORDS = D_MODEL // 2
SLABS = WORDS // LANES
STAGE_STRIDE = 136
N_STAGE = 4
VMEM_LIMIT = 56 * 1024 * 1024

_NT = (((1,), (1,)), ((), ()))


def _cparams(n_axes, vmem=VMEM_LIMIT):
    return pltpu.CompilerParams(dimension_semantics=("arbitrary",) * n_axes, vmem_limit_bytes=vmem)


def _gelu(x):
    return 0.5 * x * (1.0 + lax.erf(x * (1.0 / math.sqrt(2.0))))


def _affine(x, scale, shift=None):
    rows, c = x.shape
    xr = x.reshape(rows // GROUP, GROUP, c) * scale[None]
    if shift is not None:
        xr = xr + shift[None]
    return xr.reshape(rows, c)


def _layer_norm(z, g, b):
    mu = jnp.mean(z, axis=-1, keepdims=True)
    zc = z - mu
    var = jnp.mean(zc * zc, axis=-1, keepdims=True)
    return zc * lax.rsqrt(var + LN_EPS) * g + b


def _mod_kernel(c_ref, w_ref, b_ref, o_ref):
    c = c_ref[...]
    a = (c * jax.nn.sigmoid(c)).astype(BF16)
    o_ref[0] = jnp.dot(a, w_ref[0].astype(BF16), preferred_element_type=F32) + b_ref[0]


def _mod_call(c_all, w_ada, b_ada):
    nb = c_all.shape[0]
    tn = 1536
    return pl.pallas_call(
        _mod_kernel,
        grid=(DEPTH, 6 * D_MODEL // tn),
        in_specs=[
            pl.BlockSpec((nb, D_MODEL), lambda l, j: (0, 0)),
            pl.BlockSpec((1, D_MODEL, tn), lambda l, j: (l, 0, j)),
            pl.BlockSpec((1, 1, tn), lambda l, j: (l, 0, j)),
        ],
        out_specs=pl.BlockSpec((1, nb, tn), lambda l, j: (l, 0, j)),
        out_shape=jax.ShapeDtypeStruct((DEPTH, nb, 6 * D_MODEL), F32),
        compiler_params=_cparams(2),
        name="adaln_mod",
    )(c_all, w_ada, b_ada.reshape(DEPTH, 1, 6 * D_MODEL))


def _mm_mod_kernel(x_ref, sc_ref, sh_ref, w_ref, *rest, glu):
    o_ref = rest[-1]
    xm = _affine(x_ref[...], 1.0 + sc_ref[0], sh_ref[0])
    p = jnp.dot(xm.astype(BF16), w_ref[...], preferred_element_type=F32)
    if len(rest) == 2:
        p = p + rest[0][...]
    if glu:
        n = p.shape[-1] // 2
        p = p[:, :n] * jax.nn.sigmoid(p[:, n:])
    o_ref[...] = p


def _mm_mod_call(x, sc, sh, w_bf16, bias, n_prompt_tiles, glu, name):
    t = x.shape[0]
    n = w_bf16.shape[1]
    n_out = n // 2 if glu else n
    sel = lambda i: (jnp.where(i >= n_prompt_tiles, 1, 0), 0, 0)
    in_specs = [
        pl.BlockSpec((TM, D_MODEL), lambda i: (i, 0)),
        pl.BlockSpec((1, GROUP, D_MODEL), sel),
        pl.BlockSpec((1, GROUP, D_MODEL), sel),
        pl.BlockSpec((D_MODEL, n), lambda i: (0, 0)),
    ]
    args = [x, sc, sh, w_bf16]
    if bias is not None:
        in_specs.append(pl.BlockSpec((1, n), lambda i: (0, 0)))
        args.append(bias.reshape(1, n))
    return pl.pallas_call(
        functools.partial(_mm_mod_kernel, glu=glu),
        grid=(t // TM,),
        in_specs=in_specs,
        out_specs=pl.BlockSpec((TM, n_out), lambda i: (i, 0)),
        out_shape=jax.ShapeDtypeStruct((t, n_out), F32),
        compiler_params=_cparams(1),
        name=name,
    )(*args)


def _lru_kernel(proj_ref, conv0_ref, h0_ref, cw_ref, cb_ref, wa_ref, ba_ref, wi_ref, bi_ref, lam_ref,
                yg_ref, hnew_ref, convnew_ref, win_ref, xc_ref, a_ref, b_ref, h_ref, *, nb, tl, cw, n_tiles):
    rows = tl * nb
    halo = (LRU_CONV_W - 1) * nb

    @pl.when(pl.program_id(0) == 0)
    def _():
        win_ref[0:halo, :] = conv0_ref[...]
        h_ref[...] = h0_ref[...]

    win_ref[halo:halo + rows, :] = proj_ref[:, :D_RNN]
    xc = cb_ref[...] + win_ref[0:rows, :] * cw_ref[0:1, :]
    for w in range(1, LRU_CONV_W):
        xc = xc + win_ref[w * nb:w * nb + rows, :] * cw_ref[w:w + 1, :]
    xc_ref[...] = xc
    carry = win_ref[rows:rows + halo, :]
    convnew_ref[...] = carry
    if n_tiles > 1:
        win_ref[0:halo, :] = carry

    lam = lam_ref[...]
    neg = -lam
    softplus = jnp.maximum(neg, 0.0) + jnp.log1p(jnp.exp(-jnp.abs(neg)))
    for hd in range(N_RNN_HEADS):
        cs = slice(hd * RNN_HEAD_DIM, (hd + 1) * RNN_HEAD_DIM)
        xh = xc_ref[:, cs]
        xb = xh.astype(BF16)
        r = jax.nn.sigmoid(jnp.dot(xb, wa_ref[hd], preferred_element_type=F32) + ba_ref[:, cs])
        ig = jax.nn.sigmoid(jnp.dot(xb, wi_ref[hd], preferred_element_type=F32) + bi_ref[:, cs])
        log_a = (-LRU_C * r) * softplus[:, cs]
        th = jnp.tanh(log_a)
        a_ref[:, cs] = jnp.exp(log_a)
        b_ref[:, cs] = jnp.sqrt(-2.0 * th / (1.0 - th)) * (ig * xh)

    for c0 in range(0, D_RNN, cw):
        cs = slice(c0, c0 + cw)

        def step(l, h, cs=cs):
            rs = pl.ds(pl.multiple_of(l * nb, nb), nb)
            h = a_ref[rs, cs] * h + b_ref[rs, cs]
            b_ref[rs, cs] = h
            return h

        h_ref[:, cs] = lax.fori_loop(0, tl, step, h_ref[:, cs])

    yg_ref[...] = b_ref[...] * _gelu(proj_ref[:, D_RNN:])
    hnew_ref[...] = h_ref[...]


def _lru_call(proj, conv0, h0, cw, cb, wa, ba, wi, bi, lam, *, nb, seq, row_block0, name):
    tl = TM // nb
    n_tiles = seq // tl
    halo = (LRU_CONV_W - 1) * nb
    chan = D_RNN if nb * D_RNN <= 16 * 1024 else LANES
    full = lambda shape: pl.BlockSpec(shape, lambda i: (0,) * len(shape))
    return pl.pallas_call(
        functools.partial(_lru_kernel, nb=nb, tl=tl, cw=chan, n_tiles=n_tiles),
        grid=(n_tiles,),
        in_specs=[
            pl.BlockSpec((TM, 2 * D_RNN), lambda i: (row_block0 + i, 0)),
            full((halo, D_RNN)),
            full((nb, D_RNN)),
            full((LRU_CONV_W, D_RNN)),
            full((1, D_RNN)),
            full((N_RNN_HEADS, RNN_HEAD_DIM, RNN_HEAD_DIM)),
            full((1, D_RNN)),
            full((N_RNN_HEADS, RNN_HEAD_DIM, RNN_HEAD_DIM)),
            full((1, D_RNN)),
            full((1, D_RNN)),
        ],
        out_specs=[
            pl.BlockSpec((TM, D_RNN), lambda i: (i, 0)),
            full((nb, D_RNN)),
            full((halo, D_RNN)),
        ],
        out_shape=[
            jax.ShapeDtypeStruct((seq * nb, D_RNN), F32),
            jax.ShapeDtypeStruct((nb, D_RNN), F32),
            jax.ShapeDtypeStruct((halo, D_RNN), F32),
        ],
        scratch_shapes=[
            pltpu.VMEM((halo + TM, D_RNN), F32),
            pltpu.VMEM((TM, D_RNN), F32),
            pltpu.VMEM((TM, D_RNN), F32),
            pltpu.VMEM((TM, D_RNN), F32),
            pltpu.VMEM((nb, D_RNN), F32),
        ],
        compiler_params=_cparams(1),
        name=name,
    )(proj, conv0, h0, cw, cb.reshape(1, D_RNN), wa, ba.reshape(1, D_RNN), wi, bi.reshape(1, D_RNN),
      lam.reshape(1, D_RNN))


def _conf_kernel(u_ref, conv0_ref, dw_ref, db_ref, g_ref, b_ref, o_ref, convnew_ref, win_ref, d_ref,
                 *, nb, tl, n_tiles):
    rows = tl * nb
    halo = (CONF_CONV_W - 1) * nb
    rc = 64
    cc = 256

    @pl.when(pl.program_id(0) == 0)
    def _():
        win_ref[0:halo, :] = conv0_ref[...]

    win_ref[halo:halo + rows, :] = u_ref[...]

    for c0 in range(0, D_CONF, cc):
        cs = slice(c0, c0 + cc)

        def chunk(r, carry, cs=cs):
            r0 = pl.multiple_of(r * rc, rc)
            acc = jnp.broadcast_to(db_ref[:, cs], (rc, cc))
            for w in range(CONF_CONV_W):
                acc = acc + win_ref[pl.ds(r0 + w * nb, rc), cs] * dw_ref[w:w + 1, cs]
            d_ref[pl.ds(r0, rc), cs] = acc
            return carry

        lax.fori_loop(0, rows // rc, chunk, 0)

    convnew_ref[...] = win_ref[rows:rows + halo, :]
    if n_tiles > 1:
        win_ref[0:halo, :] = win_ref[rows:rows + halo, :]

    y = _layer_norm(d_ref[...], g_ref[...], b_ref[...])
    o_ref[...] = y * jax.nn.sigmoid(y)


def _conf_call(u, conv0, dw, db, g, b, *, nb, seq, row_block0, name):
    tl = TM // nb
    n_tiles = seq // tl
    halo = (CONF_CONV_W - 1) * nb
    full = lambda shape: pl.BlockSpec(shape, lambda i: (0,) * len(shape))
    return pl.pallas_call(
        functools.partial(_conf_kernel, nb=nb, tl=tl, n_tiles=n_tiles),
        grid=(n_tiles,),
        in_specs=[
            pl.BlockSpec((TM, D_CONF), lambda i: (row_block0 + i, 0)),
            full((halo, D_CONF)),
            full((CONF_CONV_W, D_CONF)),
            full((1, D_CONF)),
            full((1, D_CONF)),
            full((1, D_CONF)),
        ],
        out_specs=[
            pl.BlockSpec((TM, D_CONF), lambda i: (i, 0)),
            full((halo, D_CONF)),
        ],
        out_shape=[
            jax.ShapeDtypeStruct((seq * nb, D_CONF), F32),
            jax.ShapeDtypeStruct((halo, D_CONF), F32),
        ],
        scratch_shapes=[
            pltpu.VMEM((halo + TM, D_CONF), F32),
            pltpu.VMEM((TM, D_CONF), F32),
        ],
        compiler_params=_cparams(1),
        name=name,
    )(u, conv0, dw, db.reshape(1, D_CONF), g.reshape(1, D_CONF), b.reshape(1, D_CONF))


def _proj_ln_kernel(y_ref, w_ref, x_ref, gm_ref, lng_ref, lnb_ref, scf_ref, shf_ref, *rest):
    x1_ref, hf_ref = rest[-2:]
    out = jnp.dot(y_ref[...].astype(BF16), w_ref[...], preferred_element_type=F32)
    if len(rest) == 3:
        out = out + rest[0][...]
    z = DEEPNORM_ALPHA * x_ref[...] + _affine(out, gm_ref[0])
    x1 = _layer_norm(z, lng_ref[...], lnb_ref[...])
    x1_ref[...] = x1
    hf_ref[...] = _affine(x1, 1.0 + scf_ref[0], shf_ref[0])


def _proj_ln_call(y, w_bf16, bias, x, gm, lng, lnb, scf, shf, n_prompt_tiles, name):
    t, k = y.shape
    sel = lambda i: (jnp.where(i >= n_prompt_tiles, 1, 0), 0, 0)
    row = pl.BlockSpec((1, D_MODEL), lambda i: (0, 0))
    mod = pl.BlockSpec((1, GROUP, D_MODEL), sel)
    tile = pl.BlockSpec((TM, D_MODEL), lambda i: (i, 0))
    in_specs = [
        pl.BlockSpec((TM, k), lambda i: (i, 0)),
        pl.BlockSpec((k, D_MODEL), lambda i: (0, 0)),
        tile, mod, row, row, mod, mod,
    ]
    args = [y, w_bf16, x, gm, lng.reshape(1, D_MODEL), lnb.reshape(1, D_MODEL), scf, shf]
    if bias is not None:
        in_specs.append(row)
        args.append(bias.reshape(1, D_MODEL))
    return pl.pallas_call(
        _proj_ln_kernel,
        grid=(t // TM,),
        in_specs=in_specs,
        out_specs=[tile, tile],
        out_shape=[jax.ShapeDtypeStruct((t, D_MODEL), F32)] * 2,
        compiler_params=_cparams(1),
        name=name,
    )(*args)


SUB = 8

_CELLS = [(i, j) for i in range(TOPK) for j in range(TOPK) if (i + 1) * (j + 1) <= TOPK]
_CELLS = _CELLS + [None] * (-len(_CELLS) % SUB)
_CELL_GROUPS = [_CELLS[k:k + SUB] for k in range(0, len(_CELLS), SUB)]


def _stack_rows(rows, r_iota):
    out = jnp.broadcast_to(rows[-1], r_iota.shape)
    for r in range(SUB - 2, -1, -1):
        if rows[r] is not rows[r + 1]:
            out = jnp.where(r_iota <= r, rows[r], out)
    return out


def _row_iota(shape):
    return lax.broadcasted_iota(jnp.int32, shape, 0).astype(F32)


def _top_rows(s, k):
    n = s.shape[0]
    iota = _row_iota(s.shape)
    vals, idxs = [], []
    for _ in range(k):
        m = jnp.max(s, axis=0, keepdims=True)
        am = jnp.min(jnp.where(s == m, iota, float(n)), axis=0, keepdims=True)
        vals.append(m)
        idxs.append(am)
        s = jnp.where(iota == am, -jnp.inf, s)
    return vals, idxs


def _retrieve_kernel(hf_ref, wq_ref, keys_ref, idx_ref, g_ref, g_t_ref):
    q = jnp.dot(hf_ref[...].astype(BF16), wq_ref[...], preferred_element_type=F32).astype(BF16)
    r_iota = lax.broadcasted_iota(jnp.int32, (SUB, TT), 0)
    neg_inf = jnp.full((1, TT), -jnp.inf, F32)
    zero = jnp.zeros((1, TT), F32)
    for h in range(N_PEER_HEADS):
        halves = []
        for p in range(2):
            hp = h * 2 + p
            s = lax.dot_general(keys_ref[hp], q[:, hp * HALF_KEY:(hp + 1) * HALF_KEY], _NT,
                                preferred_element_type=F32)
            halves.append(_top_rows(s, TOPK))
        (v1, i1), (v2, i2) = halves
        i1 = [iv * float(N_KEYS) for iv in i1]
        cand, cidx = [], []
        for cells in _CELL_GROUPS:
            pick = lambda rows, pos, pad: _stack_rows([pad if c is None else rows[c[pos]] for c in cells], r_iota)
            cand.append(pick(v1, 0, neg_inf) + pick(v2, 1, neg_inf))
            cidx.append(pick(i1, 0, zero) + pick(i2, 1, zero))
        cand = jnp.concatenate(cand, axis=0)
        cidx = jnp.concatenate(cidx, axis=0)
        iota = _row_iota(cand.shape)
        top_s, experts = [], []
        for _ in range(TOPK):
            m = jnp.max(cand, axis=0, keepdims=True)
            am = jnp.min(jnp.where(cand == m, iota, float(len(_CELLS))), axis=0, keepdims=True)
            hit = iota == am
            experts.append(jnp.max(jnp.where(hit, cidx, -1.0), axis=0, keepdims=True))
            top_s.append(m)
            cand = jnp.where(hit, -jnp.inf, cand)
        e = [jnp.exp(_stack_rows(top_s[k:k + SUB], r_iota) - top_s[0]) for k in range(0, TOPK, SUB)]
        denom = jnp.sum(sum(e[1:], e[0]), axis=0, keepdims=True)
        for k, ev in enumerate(e):
            rows = slice(h * TOPK + k * SUB, h * TOPK + (k + 1) * SUB)
            g_t_ref[rows, :] = ev / denom
            ids = _stack_rows(experts[k * SUB:(k + 1) * SUB], r_iota)
            idx_ref[rows, :] = ids.astype(jnp.int32) * SLABS
    g_ref[...] = g_t_ref[...].T


def _retrieve_call(hf, wq_bf16, keys_bf16):
    t = hf.shape[0]
    nq = wq_bf16.shape[1]
    return pl.pallas_call(
        _retrieve_kernel,
        grid=(t // TT,),
        in_specs=[
            pl.BlockSpec((TT, D_MODEL), lambda i: (i, 0)),
            pl.BlockSpec((D_MODEL, nq), lambda i: (0, 0)),
            pl.BlockSpec((2 * N_PEER_HEADS, N_KEYS, HALF_KEY), lambda i: (0, 0, 0)),
        ],
        out_specs=[
            pl.BlockSpec((N_SEL, TT), lambda i: (0, i)),
            pl.BlockSpec((TT, N_SEL), lambda i: (i, 0)),
        ],
        out_shape=[
            jax.ShapeDtypeStruct((N_SEL, t), jnp.int32),
            jax.ShapeDtypeStruct((t, N_SEL), F32),
        ],
        scratch_shapes=[pltpu.VMEM((N_SEL, TT), F32)],
        compiler_params=_cparams(1),
        name="peer_retrieve",
    )(hf, wq_bf16, keys_bf16)


def _pack_kernel(t_ref, o_ref):
    x = t_ref[0]
    w = pltpu.pack_elementwise([x[:, :WORDS], x[:, WORDS:]], packed_dtype=BF16)
    for j in range(SLABS):
        o_ref[pl.ds(j, TM, stride=SLABS), :] = w[:, j * LANES:(j + 1) * LANES]


def _pack_table(tabs, layer):
    return pl.pallas_call(
        _pack_kernel,
        grid=(N_EXPERTS // TM,),
        in_specs=[pl.BlockSpec((1, TM, D_MODEL), lambda i: (layer, i, 0))],
        out_specs=pl.BlockSpec((TM * SLABS, LANES), lambda i: (i, 0)),
        out_shape=jax.ShapeDtypeStruct((N_EXPERTS * SLABS, LANES), jnp.uint32),
        compiler_params=_cparams(1),
        name="peer_pack_table",
    )(tabs)


def _gather_token(idx_ref, t, tab_ref, stage_ref):
    for k in range(N_SEL):
        i = pl.multiple_of(idx_ref.at[k][t], SLABS)
        stage_ref[pl.ds(k, SLABS, stride=STAGE_STRIDE), :] = tab_ref[pl.ds(i, SLABS), :]


def _unpack_words(w, half):
    return pltpu.unpack_elementwise(w, index=half, packed_dtype=BF16, unpacked_dtype=F32)


def _unpack_stage(stage_ref):
    his, los = [], []
    for j in range(SLABS):
        w = stage_ref[j * STAGE_STRIDE:j * STAGE_STRIDE + N_SEL, :]
        his.append(_unpack_words(w, 0).astype(BF16))
        los.append(_unpack_words(w, 1).astype(BF16))
    return jnp.concatenate(his + los, axis=1)


def _with_index_tile(idx_hbm, bufs, sems, run):
    step, n_steps = pl.program_id(0), pl.num_programs(0)

    def copy(tile, slot):
        cols = pl.ds(pl.multiple_of(tile * TT, TT), TT)
        return pltpu.make_async_copy(idx_hbm.at[:, cols], bufs[slot], sems.at[slot])

    @pl.when(step == 0)
    def _():
        copy(0, 0).start()

    for slot in range(2):
        @pl.when(lax.rem(step, 2) == slot)
        def _(slot=slot):
            copy(step, slot).wait()

            @pl.when(step + 1 < n_steps)
            def _():
                copy(step + 1, 1 - slot).start()

            run(bufs[slot])


def _token_pipeline(idx_ref, tab_ref, stages, produce, consume):
    for q in range(N_STAGE):
        _gather_token(idx_ref, q, tab_ref, stages[q])

    def body(i, carry):
        t0 = i * N_STAGE
        results = [produce(t0 + q, stages[q]) for q in range(N_STAGE)]
        for q in range(N_STAGE):
            _gather_token(idx_ref, jnp.minimum(t0 + N_STAGE + q, TT - 1), tab_ref, stages[q])
        for q in range(N_STAGE):
            consume(t0 + q, results[q])
        return carry

    lax.fori_loop(0, TT // N_STAGE, body, 0)


def _peer_u_kernel(idx_hbm, x_ref, tab_ref, act_ref, xb_ref, act_t_ref, idx0, idx1, sems, *stages):
    xb_ref[...] = x_ref[...].astype(BF16).astype(F32)
    act_t_ref[...] = jnp.zeros((N_SEL, TT), F32)
    lane = lax.broadcasted_iota(jnp.int32, (N_SEL, TT), 1)

    def produce(t, stage):
        xr = xb_ref[pl.ds(t, 1), :]
        p = None
        for j in range(SLABS):
            w = stage[j * STAGE_STRIDE:j * STAGE_STRIDE + N_SEL, :]
            term = (_unpack_words(w, 0) * xr[:, j * LANES:(j + 1) * LANES]
                    + _unpack_words(w, 1) * xr[:, WORDS + j * LANES:WORDS + (j + 1) * LANES])
            p = term if p is None else p + term
        return jnp.sum(p, axis=1, keepdims=True)

    def consume(t, r):
        act_t_ref[...] = jnp.where(lane == t, r, act_t_ref[...])

    _with_index_tile(idx_hbm, (idx0, idx1), sems,
                     lambda idx_ref: _token_pipeline(idx_ref, tab_ref, stages, produce, consume))
    act_ref[...] = act_t_ref[...].T


_INDEX_SCRATCH = [pltpu.SMEM((N_SEL, TT), jnp.int32), pltpu.SMEM((N_SEL, TT), jnp.int32),
                  pltpu.SemaphoreType.DMA((2,))]
_STAGE_SCRATCH = [pltpu.VMEM((SLABS * STAGE_STRIDE, LANES), jnp.uint32)] * N_STAGE


def _peer_u_call(idx, hf, tab):
    t = hf.shape[0]
    return pl.pallas_call(
        _peer_u_kernel,
        grid=(t // TT,),
        in_specs=[
            pl.BlockSpec(memory_space=pl.ANY),
            pl.BlockSpec((TT, D_MODEL), lambda i: (i, 0)),
            pl.BlockSpec(memory_space=pltpu.VMEM),
        ],
        out_specs=pl.BlockSpec((TT, N_SEL), lambda i: (i, 0)),
        out_shape=jax.ShapeDtypeStruct((t, N_SEL), F32),
        scratch_shapes=[pltpu.VMEM((TT, D_MODEL), F32), pltpu.VMEM((N_SEL, TT), F32)]
        + _INDEX_SCRATCH + _STAGE_SCRATCH,
        compiler_params=_cparams(1),
        name="peer_u",
    )(idx, hf, tab)


def _peer_v_kernel(idx_hbm, act_ref, g_ref, tab_ref, x_ref, gf_ref, lng_ref, lnb_ref, o_ref, w_ref, f_ref,
                   idx0, idx1, sems, *stages):
    w_ref[...] = g_ref[...] * _gelu(act_ref[...])

    def produce(t, stage):
        wr = jnp.broadcast_to(w_ref[pl.ds(t, 1), :], (8, N_SEL)).astype(BF16)
        return jnp.dot(wr, _unpack_stage(stage), preferred_element_type=F32)

    def consume(t, f):
        f_ref[pl.ds(t, 1), :] = f[0:1]

    _with_index_tile(idx_hbm, (idx0, idx1), sems,
                     lambda idx_ref: _token_pipeline(idx_ref, tab_ref, stages, produce, consume))
    z = DEEPNORM_ALPHA * x_ref[...] + gf_ref[0] * f_ref[...]
    o_ref[...] = _layer_norm(z, lng_ref[...], lnb_ref[...])


def _peer_v_call(idx, act, g, tab, x, gf, lng, lnb, n_prompt_tiles):
    t = x.shape[0]
    sel = lambda i: (jnp.where(i >= n_prompt_tiles, 1, 0), 0, 0)
    row = pl.BlockSpec((1, D_MODEL), lambda i: (0, 0))
    return pl.pallas_call(
        _peer_v_kernel,
        grid=(t // TT,),
        in_specs=[
            pl.BlockSpec(memory_space=pl.ANY),
            pl.BlockSpec((TT, N_SEL), lambda i: (i, 0)),
            pl.BlockSpec((TT, N_SEL), lambda i: (i, 0)),
            pl.BlockSpec(memory_space=pltpu.VMEM),
            pl.BlockSpec((TT, D_MODEL), lambda i: (i, 0)),
            pl.BlockSpec((1, GROUP, D_MODEL), sel),
            row, row,
        ],
        out_specs=pl.BlockSpec((TT, D_MODEL), lambda i: (i, 0)),
        out_shape=jax.ShapeDtypeStruct((t, D_MODEL), F32),
        scratch_shapes=[pltpu.VMEM((TT, N_SEL), F32), pltpu.VMEM((TT, D_MODEL), F32)]
        + _INDEX_SCRATCH + _STAGE_SCRATCH,
        compiler_params=_cparams(1),
        name="peer_v",
    )(idx, act, g, tab, x, gf, lng.reshape(1, D_MODEL), lnb.reshape(1, D_MODEL))


def _time_major(a):
    return jnp.swapaxes(a, 0, 1).reshape(a.shape[0] * a.shape[1], a.shape[2])


def _batch_major(a, nb):
    return jnp.swapaxes(a.reshape(a.shape[0] // nb, nb, a.shape[1]), 0, 1)


def kernel(x_prompt, x_sample, state_lru_h, state_lru_conv, state_conf_conv, c_prompt, c_sample, w_ada, b_ada, ln_mix_g, ln_mix_b, ln_ffn_g, ln_ffn_b, lru_w_in, lru_conv_w, lru_conv_b, lru_w_a, lru_b_a, lru_w_i, lru_b_i, lru_lambda, lru_w_out, conf_w_pw1, conf_b_pw1, conf_dw_w, conf_dw_b, conf_ln_g, conf_ln_b, conf_w_pw2, conf_b_pw2, peer_w_q, peer_keys, peer_u, peer_v):
    bp, lp, _ = x_prompt.shape
    bs, ls, _ = x_sample.shape
    tp, ts = bp * lp, bs * ls
    assert GROUP % bp == 0 and bs == GROUP and TM % GROUP == 0
    for nb, seq in ((bp, lp), (bs, ls)):
        assert (seq * nb) % TM == 0 and (seq * nb == TM or TM // nb >= CONF_CONV_W - 1)
    npt_m, npt_t = tp // TM, tp // TT

    x = jnp.concatenate([_time_major(x_prompt), _time_major(x_sample)], axis=0)

    mod = _mod_call(jnp.concatenate([c_prompt, c_sample], axis=0), w_ada, b_ada)
    mod = mod.reshape(DEPTH, bp + bs, 6, D_MODEL)
    pat = jnp.stack([jnp.tile(mod[:, :bp], (1, GROUP // bp, 1, 1)), mod[:, bp:]], axis=1)
    pat = jnp.transpose(pat, (0, 3, 1, 2, 4))
    mods = [[pat[i, k] for k in range(6)] for i in range(DEPTH)]

    zeros = lambda *s: jnp.zeros(s, F32)
    new_h, new_lconv, new_cconv = [], [], []
    for i in range(DEPTH):
        sh_m, sc_m, g_m, sh_f, sc_f, g_f = mods[i]
        j = i // 2
        if i % 2 == 0:
            proj = _mm_mod_call(x, sc_m, sh_m, lru_w_in[j].astype(BF16), None, npt_m, False, "lru_in_proj")
            wa, wi = lru_w_a[j].astype(BF16), lru_w_i[j].astype(BF16)
            common = (lru_conv_w[j], lru_conv_b[j], wa, lru_b_a[j], wi, lru_b_i[j], lru_lambda[j])
            yp, hp, cp = _lru_call(proj, zeros((LRU_CONV_W - 1) * bp, D_RNN), zeros(bp, D_RNN), *common,
                                   nb=bp, seq=lp, row_block0=0, name="lru_seq_prompt")
            ys, hs, cs = _lru_call(proj, _time_major(state_lru_conv[j]), state_lru_h[j], *common,
                                   nb=bs, seq=ls, row_block0=npt_m, name="lru_seq_sample")
            new_h.append((hp, hs))
            new_lconv.append((_batch_major(cp, bp), _batch_major(cs, bs)))
            mixed, w_out, b_out = jnp.concatenate([yp, ys], axis=0), lru_w_out[j], None
        else:
            u = _mm_mod_call(x, sc_m, sh_m, conf_w_pw1[j].astype(BF16), conf_b_pw1[j], npt_m, True, "conf_pw1_glu")
            common = (conf_dw_w[j], conf_dw_b[j], conf_ln_g[j], conf_ln_b[j])
            dp, cp = _conf_call(u, zeros((CONF_CONV_W - 1) * bp, D_CONF), *common,
                                nb=bp, seq=lp, row_block0=0, name="conf_seq_prompt")
            ds, cs = _conf_call(u, _time_major(state_conf_conv[j]), *common,
                                nb=bs, seq=ls, row_block0=npt_m, name="conf_seq_sample")
            new_cconv.append((_batch_major(cp, bp), _batch_major(cs, bs)))
            mixed, w_out, b_out = jnp.concatenate([dp, ds], axis=0), conf_w_pw2[j], conf_b_pw2[j]

        x, hf = _proj_ln_call(mixed, w_out.astype(BF16), b_out, x, g_m, ln_mix_g[i], ln_mix_b[i], sc_f, sh_f,
                              npt_m, "mix_out_ln")
        idx, gate = _retrieve_call(hf, peer_w_q[i].astype(BF16),
                                   peer_keys[i].astype(BF16).reshape(2 * N_PEER_HEADS, N_KEYS, HALF_KEY))
        act = _peer_u_call(idx, hf, _pack_table(peer_u, i))
        x = _peer_v_call(idx, act, gate, _pack_table(peer_v, i), x, g_f, ln_ffn_g[i], ln_ffn_b[i], npt_t)

    y_prompt = _batch_major(x[:tp], bp)
    y_sample = _batch_major(x[tp:], bs)
    pick = lambda pairs, k: jnp.stack([p[k] for p in pairs])
    return (y_prompt, y_sample, pick(new_h, 0), pick(new_lconv, 0), pick(new_cconv, 0),
            pick(new_h, 1), pick(new_lconv, 1), pick(new_cconv, 1))
```

```python
import functools
import math

import jax
import jax.numpy as jnp
from jax import lax
from jax.experimental import pallas as pl
from jax.experimental.pallas import tpu as pltpu

F32 = jnp.float32
BF16 = jnp.bfloat16

D_MODEL = 1024
DEPTH = 2
D_RNN = 1280
N_RNN_HEADS = 10
RNN_HEAD_DIM = 128
LRU_CONV_W = 4
LRU_C = 8.0
D_CONF = 1024
CONF_CONV_W = 31
N_PEER_HEADS = 8
N_KEYS = 128
N_EXPERTS = N_KEYS * N_KEYS
HALF_KEY = 128
TOPK = 16
N_SEL = N_PEER_HEADS * TOPK
DEEPNORM_ALPHA = (2 * DEPTH) ** 0.25
LN_EPS = 1e-5

LANES = 128
GROUP = 128
TM = 512
TT = 128
WORDS = D_MODEL // 2
SLABS = WORDS // LANES
N_STAGE = 4
VMEM_LIMIT = 56 * 1024 * 1024

_NT = (((1,), (1,)), ((), ()))


def _cparams(n_axes, vmem=VMEM_LIMIT):
    return pltpu.CompilerParams(dimension_semantics=("arbitrary",) * n_axes, vmem_limit_bytes=vmem)


def _gelu(x):
    return 0.5 * x * (1.0 + lax.erf(x * (1.0 / math.sqrt(2.0))))


def _affine(x, scale, shift=None):
    rows, c = x.shape
    xr = x.reshape(rows // GROUP, GROUP, c) * scale[None]
    if shift is not None:
        xr = xr + shift[None]
    return xr.reshape(rows, c)


def _layer_norm(z, g, b):
    mu = jnp.mean(z, axis=-1, keepdims=True)
    zc = z - mu
    var = jnp.mean(zc * zc, axis=-1, keepdims=True)
    return zc * lax.rsqrt(var + LN_EPS) * g + b


def _mod_kernel(c_ref, w_ref, b_ref, o_ref):
    c = c_ref[...]
    a = (c * jax.nn.sigmoid(c)).astype(BF16)
    o_ref[0] = jnp.dot(a, w_ref[0].astype(BF16), preferred_element_type=F32) + b_ref[0]


def _mod_call(c_all, w_ada, b_ada):
    nb = c_all.shape[0]
    tn = 1536
    return pl.pallas_call(
        _mod_kernel,
        grid=(DEPTH, 6 * D_MODEL // tn),
        in_specs=[
            pl.BlockSpec((nb, D_MODEL), lambda l, j: (0, 0)),
            pl.BlockSpec((1, D_MODEL, tn), lambda l, j: (l, 0, j)),
            pl.BlockSpec((1, 1, tn), lambda l, j: (l, 0, j)),
        ],
        out_specs=pl.BlockSpec((1, nb, tn), lambda l, j: (l, 0, j)),
        out_shape=jax.ShapeDtypeStruct((DEPTH, nb, 6 * D_MODEL), F32),
        compiler_params=_cparams(2),
        name="adaln_mod",
    )(c_all, w_ada, b_ada.reshape(DEPTH, 1, 6 * D_MODEL))


def _mm_mod_kernel(x_ref, sc_ref, sh_ref, w_ref, *rest, glu):
    o_ref = rest[-1]
    xm = _affine(x_ref[...], 1.0 + sc_ref[0], sh_ref[0])
    p = jnp.dot(xm.astype(BF16), w_ref[...], preferred_element_type=F32)
    if len(rest) == 2:
        p = p + rest[0][...]
    if glu:
        n = p.shape[-1] // 2
        p = p[:, :n] * jax.nn.sigmoid(p[:, n:])
    o_ref[...] = p


def _mm_mod_call(x, sc, sh, w_bf16, bias, n_prompt_tiles, glu, name):
    t = x.shape[0]
    n = w_bf16.shape[1]
    n_out = n // 2 if glu else n
    sel = lambda i: (jnp.where(i >= n_prompt_tiles, 1, 0), 0, 0)
    in_specs = [
        pl.BlockSpec((TM, D_MODEL), lambda i: (i, 0)),
        pl.BlockSpec((1, GROUP, D_MODEL), sel),
        pl.BlockSpec((1, GROUP, D_MODEL), sel),
        pl.BlockSpec((D_MODEL, n), lambda i: (0, 0)),
    ]
    args = [x, sc, sh, w_bf16]
    if bias is not None:
        in_specs.append(pl.BlockSpec((1, n), lambda i: (0, 0)))
        args.append(bias.reshape(1, n))
    return pl.pallas_call(
        functools.partial(_mm_mod_kernel, glu=glu),
        grid=(t // TM,),
        in_specs=in_specs,
        out_specs=pl.BlockSpec((TM, n_out), lambda i: (i, 0)),
        out_shape=jax.ShapeDtypeStruct((t, n_out), F32),
        compiler_params=_cparams(1),
        name=name,
    )(*args)


def _lru_kernel(proj_ref, conv0_ref, h0_ref, cw_ref, cb_ref, wa_ref, ba_ref, wi_ref, bi_ref, lam_ref,
                yg_ref, hnew_ref, convnew_ref, win_ref, xc_ref, a_ref, b_ref, h_ref, *, nb, tl, cw, n_tiles):
    rows = tl * nb
    halo = (LRU_CONV_W - 1) * nb

    @pl.when(pl.program_id(0) == 0)
    def _():
        win_ref[0:halo, :] = conv0_ref[...]
        h_ref[...] = h0_ref[...]

    win_ref[halo:halo + rows, :] = proj_ref[:, :D_RNN]
    xc = cb_ref[...] + win_ref[0:rows, :] * cw_ref[0:1, :]
    for w in range(1, LRU_CONV_W):
        xc = xc + win_ref[w * nb:w * nb + rows, :] * cw_ref[w:w + 1, :]
    xc_ref[...] = xc
    carry = win_ref[rows:rows + halo, :]
    convnew_ref[...] = carry
    if n_tiles > 1:
        win_ref[0:halo, :] = carry

    lam = lam_ref[...]
    neg = -lam
    softplus = jnp.maximum(neg, 0.0) + jnp.log1p(jnp.exp(-jnp.abs(neg)))
    for hd in range(N_RNN_HEADS):
        cs = slice(hd * RNN_HEAD_DIM, (hd + 1) * RNN_HEAD_DIM)
        xh = xc_ref[:, cs]
        xb = xh.astype(BF16)
        r = jax.nn.sigmoid(jnp.dot(xb, wa_ref[hd], preferred_element_type=F32) + ba_ref[:, cs])
        ig = jax.nn.sigmoid(jnp.dot(xb, wi_ref[hd], preferred_element_type=F32) + bi_ref[:, cs])
        log_a = (-LRU_C * r) * softplus[:, cs]
        th = jnp.tanh(log_a)
        a_ref[:, cs] = jnp.exp(log_a)
        b_ref[:, cs] = jnp.sqrt(-2.0 * th / (1.0 - th)) * (ig * xh)

    for c0 in range(0, D_RNN, cw):
        cs = slice(c0, c0 + cw)

        def step(l, h, cs=cs):
            rs = pl.ds(pl.multiple_of(l * nb, nb), nb)
            h = a_ref[rs, cs] * h + b_ref[rs, cs]
            b_ref[rs, cs] = h
            return h

        h_ref[:, cs] = lax.fori_loop(0, tl, step, h_ref[:, cs])

    yg_ref[...] = b_ref[...] * _gelu(proj_ref[:, D_RNN:])
    hnew_ref[...] = h_ref[...]


def _lru_call(proj, conv0, h0, cw, cb, wa, ba, wi, bi, lam, *, nb, seq, row_block0, name):
    tl = TM // nb
    n_tiles = seq // tl
    halo = (LRU_CONV_W - 1) * nb
    chan = D_RNN if nb * D_RNN <= 16 * 1024 else LANES
    full = lambda shape: pl.BlockSpec(shape, lambda i: (0,) * len(shape))
    return pl.pallas_call(
        functools.partial(_lru_kernel, nb=nb, tl=tl, cw=chan, n_tiles=n_tiles),
        grid=(n_tiles,),
        in_specs=[
            pl.BlockSpec((TM, 2 * D_RNN), lambda i: (row_block0 + i, 0)),
            full((halo, D_RNN)),
            full((nb, D_RNN)),
            full((LRU_CONV_W, D_RNN)),
            full((1, D_RNN)),
            full((N_RNN_HEADS, RNN_HEAD_DIM, RNN_HEAD_DIM)),
            full((1, D_RNN)),
            full((N_RNN_HEADS, RNN_HEAD_DIM, RNN_HEAD_DIM)),
            full((1, D_RNN)),
            full((1, D_RNN)),
        ],
        out_specs=[
            pl.BlockSpec((TM, D_RNN), lambda i: (i, 0)),
            full((nb, D_RNN)),
            full((halo, D_RNN)),
        ],
        out_shape=[
            jax.ShapeDtypeStruct((seq * nb, D_RNN), F32),
            jax.ShapeDtypeStruct((nb, D_RNN), F32),
            jax.ShapeDtypeStruct((halo, D_RNN), F32),
        ],
        scratch_shapes=[
            pltpu.VMEM((halo + TM, D_RNN), F32),
            pltpu.VMEM((TM, D_RNN), F32),
            pltpu.VMEM((TM, D_RNN), F32),
            pltpu.VMEM((TM, D_RNN), F32),
            pltpu.VMEM((nb, D_RNN), F32),
        ],
        compiler_params=_cparams(1),
        name=name,
    )(proj, conv0, h0, cw, cb.reshape(1, D_RNN), wa, ba.reshape(1, D_RNN), wi, bi.reshape(1, D_RNN),
      lam.reshape(1, D_RNN))


def _conf_kernel(u_ref, conv0_ref, dw_ref, db_ref, g_ref, b_ref, o_ref, convnew_ref, win_ref, d_ref,
                 *, nb, tl, n_tiles):
    rows = tl * nb
    halo = (CONF_CONV_W - 1) * nb
    rc = 64
    cc = 256

    @pl.when(pl.program_id(0) == 0)
    def _():
        win_ref[0:halo, :] = conv0_ref[...]

    win_ref[halo:halo + rows, :] = u_ref[...]

    for c0 in range(0, D_CONF, cc):
        cs = slice(c0, c0 + cc)

        def chunk(r, carry, cs=cs):
            r0 = pl.multiple_of(r * rc, rc)
            acc = jnp.broadcast_to(db_ref[:, cs], (rc, cc))
            for w in range(CONF_CONV_W):
                acc = acc + win_ref[pl.ds(r0 + w * nb, rc), cs] * dw_ref[w:w + 1, cs]
            d_ref[pl.ds(r0, rc), cs] = acc
            return carry

        lax.fori_loop(0, rows // rc, chunk, 0)

    convnew_ref[...] = win_ref[rows:rows + halo, :]
    if n_tiles > 1:
        win_ref[0:halo, :] = win_ref[rows:rows + halo, :]

    y = _layer_norm(d_ref[...], g_ref[...], b_ref[...])
    o_ref[...] = y * jax.nn.sigmoid(y)


def _conf_call(u, conv0, dw, db, g, b, *, nb, seq, row_block0, name):
    tl = TM // nb
    n_tiles = seq // tl
    halo = (CONF_CONV_W - 1) * nb
    full = lambda shape: pl.BlockSpec(shape, lambda i: (0,) * len(shape))
    return pl.pallas_call(
        functools.partial(_conf_kernel, nb=nb, tl=tl, n_tiles=n_tiles),
        grid=(n_tiles,),
        in_specs=[
            pl.BlockSpec((TM, D_CONF), lambda i: (row_block0 + i, 0)),
            full((halo, D_CONF)),
            full((CONF_CONV_W, D_CONF)),
            full((1, D_CONF)),
            full((1, D_CONF)),
            full((1, D_CONF)),
        ],
        out_specs=[
            pl.BlockSpec((TM, D_CONF), lambda i: (i, 0)),
            full((halo, D_CONF)),
        ],
        out_shape=[
            jax.ShapeDtypeStruct((seq * nb, D_CONF), F32),
            jax.ShapeDtypeStruct((halo, D_CONF), F32),
        ],
        scratch_shapes=[
            pltpu.VMEM((halo + TM, D_CONF), F32),
            pltpu.VMEM((TM, D_CONF), F32),
        ],
        compiler_params=_cparams(1),
        name=name,
    )(u, conv0, dw, db.reshape(1, D_CONF), g.reshape(1, D_CONF), b.reshape(1, D_CONF))


def _proj_ln_kernel(yp_ref, ys_ref, w_ref, x_ref, gm_ref, lng_ref, lnb_ref, scf_ref, shf_ref, *rest, n_prompt_tiles):
    x1_ref, hf_ref = rest[-2:]
    y = jnp.where(pl.program_id(0) < n_prompt_tiles, yp_ref[...], ys_ref[...])
    out = jnp.dot(y.astype(BF16), w_ref[...], preferred_element_type=F32)
    if len(rest) == 3:
        out = out + rest[0][...]
    z = DEEPNORM_ALPHA * x_ref[...] + _affine(out, gm_ref[0])
    x1 = _layer_norm(z, lng_ref[...], lnb_ref[...])
    x1_ref[...] = x1
    hf_ref[...] = _affine(x1, 1.0 + scf_ref[0], shf_ref[0])


def _proj_ln_call(y_prompt, y_sample, w_bf16, bias, x, gm, lng, lnb, scf, shf, n_prompt_tiles, name):
    t, k = x.shape[0], y_prompt.shape[1]
    sel = lambda i: (jnp.where(i >= n_prompt_tiles, 1, 0), 0, 0)
    row = pl.BlockSpec((1, D_MODEL), lambda i: (0, 0))
    mod = pl.BlockSpec((1, GROUP, D_MODEL), sel)
    tile = pl.BlockSpec((TM, D_MODEL), lambda i: (i, 0))
    in_specs = [
        pl.BlockSpec((TM, k), lambda i: (jnp.minimum(i, n_prompt_tiles - 1), 0)),
        pl.BlockSpec((TM, k), lambda i: (jnp.maximum(i - n_prompt_tiles, 0), 0)),
        pl.BlockSpec((k, D_MODEL), lambda i: (0, 0)),
        tile, mod, row, row, mod, mod,
    ]
    args = [y_prompt, y_sample, w_bf16, x, gm, lng.reshape(1, D_MODEL), lnb.reshape(1, D_MODEL), scf, shf]
    if bias is not None:
        in_specs.append(row)
        args.append(bias.reshape(1, D_MODEL))
    return pl.pallas_call(
        functools.partial(_proj_ln_kernel, n_prompt_tiles=n_prompt_tiles),
        grid=(t // TM,),
        in_specs=in_specs,
        out_specs=[tile, tile],
        out_shape=[jax.ShapeDtypeStruct((t, D_MODEL), F32)] * 2,
        compiler_params=_cparams(1),
        name=name,
    )(*args)


SUB = 8

_CELLS = [(i, j) for i in range(TOPK) for j in range(TOPK) if (i + 1) * (j + 1) <= TOPK]
_CELLS = _CELLS + [None] * (-len(_CELLS) % SUB)
_CELL_GROUPS = [_CELLS[k:k + SUB] for k in range(0, len(_CELLS), SUB)]


def _stack_rows(rows, r_iota):
    out = jnp.broadcast_to(rows[-1], r_iota.shape)
    for r in range(SUB - 2, -1, -1):
        if rows[r] is not rows[r + 1]:
            out = jnp.where(r_iota <= r, rows[r], out)
    return out


def _row_iota(shape):
    return lax.broadcasted_iota(jnp.int32, shape, 0).astype(F32)


def _top_rows(s, k):
    n = s.shape[0]
    iota = _row_iota(s.shape)
    vals, idxs = [], []
    for _ in range(k):
        m = jnp.max(s, axis=0, keepdims=True)
        am = jnp.min(jnp.where(s == m, iota, float(n)), axis=0, keepdims=True)
        vals.append(m)
        idxs.append(am)
        s = jnp.where(iota == am, -jnp.inf, s)
    return vals, idxs


def _retrieve_kernel(hf_ref, wq_ref, keys_ref, idx_ref, g_ref, g_t_ref):
    q = jnp.dot(hf_ref[...].astype(BF16), wq_ref[...], preferred_element_type=F32).astype(BF16)
    r_iota = lax.broadcasted_iota(jnp.int32, (SUB, TT), 0)
    neg_inf = jnp.full((1, TT), -jnp.inf, F32)
    zero = jnp.zeros((1, TT), F32)
    for h in range(N_PEER_HEADS):
        halves = []
        for p in range(2):
            hp = h * 2 + p
            s = lax.dot_general(keys_ref[hp], q[:, hp * HALF_KEY:(hp + 1) * HALF_KEY], _NT,
                                preferred_element_type=F32)
            halves.append(_top_rows(s, TOPK))
        (v1, i1), (v2, i2) = halves
        i1 = [iv * float(N_KEYS) for iv in i1]
        cand, cidx = [], []
        for cells in _CELL_GROUPS:
            pick = lambda rows, pos, pad: _stack_rows([pad if c is None else rows[c[pos]] for c in cells], r_iota)
            cand.append(pick(v1, 0, neg_inf) + pick(v2, 1, neg_inf))
            cidx.append(pick(i1, 0, zero) + pick(i2, 1, zero))
        cand = jnp.concatenate(cand, axis=0)
        cidx = jnp.concatenate(cidx, axis=0)
        iota = _row_iota(cand.shape)
        top_s, experts = [], []
        for _ in range(TOPK):
            m = jnp.max(cand, axis=0, keepdims=True)
            am = jnp.min(jnp.where(cand == m, iota, float(len(_CELLS))), axis=0, keepdims=True)
            hit = iota == am
            experts.append(jnp.max(jnp.where(hit, cidx, -1.0), axis=0, keepdims=True))
            top_s.append(m)
            cand = jnp.where(hit, -jnp.inf, cand)
        e = [jnp.exp(_stack_rows(top_s[k:k + SUB], r_iota) - top_s[0]) for k in range(0, TOPK, SUB)]
        denom = jnp.sum(sum(e[1:], e[0]), axis=0, keepdims=True)
        for k, ev in enumerate(e):
            rows = slice(h * TOPK + k * SUB, h * TOPK + (k + 1) * SUB)
            g_t_ref[rows, :] = ev / denom
            ids = _stack_rows(experts[k * SUB:(k + 1) * SUB], r_iota)
            idx_ref[rows, :] = ids.astype(jnp.int32) * SLABS
    g_ref[...] = g_t_ref[...].T


def _retrieve_call(hf, wq_bf16, keys_bf16):
    t = hf.shape[0]
    nq = wq_bf16.shape[1]
    return pl.pallas_call(
        _retrieve_kernel,
        grid=(t // TT,),
        in_specs=[
            pl.BlockSpec((TT, D_MODEL), lambda i: (i, 0)),
            pl.BlockSpec((D_MODEL, nq), lambda i: (0, 0)),
            pl.BlockSpec((2 * N_PEER_HEADS, N_KEYS, HALF_KEY), lambda i: (0, 0, 0)),
        ],
        out_specs=[
            pl.BlockSpec((N_SEL, TT), lambda i: (0, i)),
            pl.BlockSpec((TT, N_SEL), lambda i: (i, 0)),
        ],
        out_shape=[
            jax.ShapeDtypeStruct((N_SEL, t), jnp.int32),
            jax.ShapeDtypeStruct((t, N_SEL), F32),
        ],
        scratch_shapes=[pltpu.VMEM((N_SEL, TT), F32)],
        compiler_params=_cparams(1),
        name="peer_retrieve",
    )(hf, wq_bf16, keys_bf16)


def _pack_kernel(t_ref, o_ref):
    x = t_ref[0]
    w = pltpu.pack_elementwise([x[:, :WORDS], x[:, WORDS:]], packed_dtype=BF16)
    for j in range(SLABS):
        o_ref[pl.ds(j, TM, stride=SLABS), :] = w[:, j * LANES:(j + 1) * LANES]


def _pack_table(tabs, layer):
    return pl.pallas_call(
        _pack_kernel,
        grid=(N_EXPERTS // TM,),
        in_specs=[pl.BlockSpec((1, TM, D_MODEL), lambda i: (layer, i, 0))],
        out_specs=pl.BlockSpec((TM * SLABS, LANES), lambda i: (i, 0)),
        out_shape=jax.ShapeDtypeStruct((N_EXPERTS * SLABS, LANES), jnp.uint32),
        compiler_params=_cparams(1),
        name="peer_pack_table",
    )(tabs)


def _gather_token(idx_ref, t, tab_ref, stage_ref):
    for k in range(N_SEL):
        i = pl.multiple_of(idx_ref.at[k][t], SLABS)
        stage_ref[pl.ds(k * SLABS, SLABS), :] = tab_ref[pl.ds(i, SLABS), :]


def _unpack_words(w, half):
    return pltpu.unpack_elementwise(w, index=half, packed_dtype=BF16, unpacked_dtype=F32)


def _unpack_stage(stage_ref):
    his, los = [], []
    for j in range(SLABS):
        w = stage_ref[pl.ds(j, N_SEL, stride=SLABS), :]
        his.append(_unpack_words(w, 0).astype(BF16))
        los.append(_unpack_words(w, 1).astype(BF16))
    return jnp.concatenate(his + los, axis=1)


def _token_pipeline(idx_hbm, bufs, sems, tab_ref, stages, produce, consume):
    step, n_steps = pl.program_id(0), pl.num_programs(0)
    n_groups = TT // N_STAGE

    def copy(tile, slot):
        cols = pl.ds(pl.multiple_of(tile * TT, TT), TT)
        return pltpu.make_async_copy(idx_hbm.at[:, cols], bufs[slot], sems.at[slot])

    def gather_group(idx_ref, t0):
        for q in range(N_STAGE):
            _gather_token(idx_ref, t0 + q, tab_ref, stages[q])

    def group(t0, refill):
        results = [produce(t0 + q, stages[q]) for q in range(N_STAGE)]
        refill()
        for q in range(N_STAGE):
            consume(t0 + q, results[q])

    @pl.when(step == 0)
    def _():
        copy(0, 0).start()
        copy(0, 0).wait()
        gather_group(bufs[0], 0)

    for slot in range(2):
        @pl.when(lax.rem(step, 2) == slot)
        def _(slot=slot):
            has_next = step + 1 < n_steps

            @pl.when(has_next)
            def _():
                copy(step + 1, 1 - slot).start()

            def body(i, carry):
                t0 = i * N_STAGE
                group(t0, lambda: gather_group(bufs[slot], t0 + N_STAGE))
                return carry

            lax.fori_loop(0, n_groups - 1, body, 0)

            @pl.when(has_next)
            def _():
                copy(step + 1, 1 - slot).wait()
                group(TT - N_STAGE, lambda: gather_group(bufs[1 - slot], 0))

            @pl.when(jnp.logical_not(has_next))
            def _():
                group(TT - N_STAGE, lambda: None)


def _peer_u_kernel(idx_hbm, x_ref, tab_ref, act_ref, xb_ref, act_t_ref, idx0, idx1, sems, *stages):
    xb_ref[...] = x_ref[...].astype(BF16).astype(F32)
    act_t_ref[...] = jnp.zeros((N_SEL, TT), F32)
    lane = lax.broadcasted_iota(jnp.int32, (N_SEL, TT), 1)

    def produce(t, stage):
        xr = xb_ref[pl.ds(t, 1), :]
        p = None
        for j in range(SLABS):
            w = stage[pl.ds(j, N_SEL, stride=SLABS), :]
            term = (_unpack_words(w, 0) * xr[:, j * LANES:(j + 1) * LANES]
                    + _unpack_words(w, 1) * xr[:, WORDS + j * LANES:WORDS + (j + 1) * LANES])
            p = term if p is None else p + term
        return jnp.sum(p, axis=1, keepdims=True)

    def consume(t, r):
        act_t_ref[...] = jnp.where(lane == t, r, act_t_ref[...])

    _token_pipeline(idx_hbm, (idx0, idx1), sems, tab_ref, stages, produce, consume)
    act_ref[...] = act_t_ref[...].T


_INDEX_SCRATCH = [pltpu.SMEM((N_SEL, TT), jnp.int32), pltpu.SMEM((N_SEL, TT), jnp.int32),
                  pltpu.SemaphoreType.DMA((2,))]
_STAGE_SCRATCH = [pltpu.VMEM((SLABS * N_SEL, LANES), jnp.uint32)] * N_STAGE


def _peer_u_call(idx, hf, tab):
    t = hf.shape[0]
    return pl.pallas_call(
        _peer_u_kernel,
        grid=(t // TT,),
        in_specs=[
            pl.BlockSpec(memory_space=pl.ANY),
            pl.BlockSpec((TT, D_MODEL), lambda i: (i, 0)),
            pl.BlockSpec(memory_space=pltpu.VMEM),
        ],
        out_specs=pl.BlockSpec((TT, N_SEL), lambda i: (i, 0)),
        out_shape=jax.ShapeDtypeStruct((t, N_SEL), F32),
        scratch_shapes=[pltpu.VMEM((TT, D_MODEL), F32), pltpu.VMEM((N_SEL, TT), F32)]
        + _INDEX_SCRATCH + _STAGE_SCRATCH,
        compiler_params=_cparams(1),
        name="peer_u",
    )(idx, hf, tab)


def _peer_v_kernel(idx_hbm, act_ref, g_ref, tab_ref, x_ref, gf_ref, lng_ref, lnb_ref, o_ref, w_ref, f_ref,
                   idx0, idx1, sems, *stages):
    w_ref[...] = g_ref[...] * _gelu(act_ref[...])

    def produce(t, stage):
        wr = jnp.broadcast_to(w_ref[pl.ds(t, 1), :], (8, N_SEL)).astype(BF16)
        return jnp.dot(wr, _unpack_stage(stage), preferred_element_type=F32)

    def consume(t, f):
        f_ref[pl.ds(t, 1), :] = f[0:1]

    _token_pipeline(idx_hbm, (idx0, idx1), sems, tab_ref, stages, produce, consume)
    z = DEEPNORM_ALPHA * x_ref[...] + gf_ref[0] * f_ref[...]
    o_ref[...] = _layer_norm(z, lng_ref[...], lnb_ref[...])


def _peer_v_call(idx, act, g, tab, x, gf, lng, lnb, n_prompt_tiles):
    t = x.shape[0]
    sel = lambda i: (jnp.where(i >= n_prompt_tiles, 1, 0), 0, 0)
    row = pl.BlockSpec((1, D_MODEL), lambda i: (0, 0))
    return pl.pallas_call(
        _peer_v_kernel,
        grid=(t // TT,),
        in_specs=[
            pl.BlockSpec(memory_space=pl.ANY),
            pl.BlockSpec((TT, N_SEL), lambda i: (i, 0)),
            pl.BlockSpec((TT, N_SEL), lambda i: (i, 0)),
            pl.BlockSpec(memory_space=pltpu.VMEM),
            pl.BlockSpec((TT, D_MODEL), lambda i: (i, 0)),
            pl.BlockSpec((1, GROUP, D_MODEL), sel),
            row, row,
        ],
        out_specs=pl.BlockSpec((TT, D_MODEL), lambda i: (i, 0)),
        out_shape=jax.ShapeDtypeStruct((t, D_MODEL), F32),
        scratch_shapes=[pltpu.VMEM((TT, N_SEL), F32), pltpu.VMEM((TT, D_MODEL), F32)]
        + _INDEX_SCRATCH + _STAGE_SCRATCH,
        compiler_params=_cparams(1),
        name="peer_v",
    )(idx, act, g, tab, x, gf, lng.reshape(1, D_MODEL), lnb.reshape(1, D_MODEL))


def _time_major(a):
    return jnp.swapaxes(a, 0, 1).reshape(a.shape[0] * a.shape[1], a.shape[2])


def _batch_major(a, nb):
    return jnp.swapaxes(a.reshape(a.shape[0] // nb, nb, a.shape[1]), 0, 1)


def kernel(x_prompt, x_sample, state_lru_h, state_lru_conv, state_conf_conv, c_prompt, c_sample, w_ada, b_ada, ln_mix_g, ln_mix_b, ln_ffn_g, ln_ffn_b, lru_w_in, lru_conv_w, lru_conv_b, lru_w_a, lru_b_a, lru_w_i, lru_b_i, lru_lambda, lru_w_out, conf_w_pw1, conf_b_pw1, conf_dw_w, conf_dw_b, conf_ln_g, conf_ln_b, conf_w_pw2, conf_b_pw2, peer_w_q, peer_keys, peer_u, peer_v):
    bp, lp, _ = x_prompt.shape
    bs, ls, _ = x_sample.shape
    tp, ts = bp * lp, bs * ls
    assert GROUP % bp == 0 and bs == GROUP and TM % GROUP == 0
    for nb, seq in ((bp, lp), (bs, ls)):
        assert (seq * nb) % TM == 0 and (seq * nb == TM or TM // nb >= CONF_CONV_W - 1)
    npt_m, npt_t = tp // TM, tp // TT

    x = jnp.concatenate([_time_major(x_prompt), _time_major(x_sample)], axis=0)

    mod = _mod_call(jnp.concatenate([c_prompt, c_sample], axis=0), w_ada, b_ada)
    mod = mod.reshape(DEPTH, bp + bs, 6, D_MODEL)
    pat = jnp.stack([jnp.tile(mod[:, :bp], (1, GROUP // bp, 1, 1)), mod[:, bp:]], axis=1)
    pat = jnp.transpose(pat, (0, 3, 1, 2, 4))
    mods = [[pat[i, k] for k in range(6)] for i in range(DEPTH)]

    zeros = lambda *s: jnp.zeros(s, F32)
    new_h, new_lconv, new_cconv = [], [], []
    for i in range(DEPTH):
        sh_m, sc_m, g_m, sh_f, sc_f, g_f = mods[i]
        j = i // 2
        if i % 2 == 0:
            proj = _mm_mod_call(x, sc_m, sh_m, lru_w_in[j].astype(BF16), None, npt_m, False, "lru_in_proj")
            wa, wi = lru_w_a[j].astype(BF16), lru_w_i[j].astype(BF16)
            common = (lru_conv_w[j], lru_conv_b[j], wa, lru_b_a[j], wi, lru_b_i[j], lru_lambda[j])
            yp, hp, cp = _lru_call(proj, zeros((LRU_CONV_W - 1) * bp, D_RNN), zeros(bp, D_RNN), *common,
                                   nb=bp, seq=lp, row_block0=0, name="lru_seq_prompt")
            ys, hs, cs = _lru_call(proj, _time_major(state_lru_conv[j]), state_lru_h[j], *common,
                                   nb=bs, seq=ls, row_block0=npt_m, name="lru_seq_sample")
            new_h.append((hp, hs))
            new_lconv.append((_batch_major(cp, bp), _batch_major(cs, bs)))
            mixed, w_out, b_out = (yp, ys), lru_w_out[j], None
        else:
            u = _mm_mod_call(x, sc_m, sh_m, conf_w_pw1[j].astype(BF16), conf_b_pw1[j], npt_m, True, "conf_pw1_glu")
            common = (conf_dw_w[j], conf_dw_b[j], conf_ln_g[j], conf_ln_b[j])
            dp, cp = _conf_call(u, zeros((CONF_CONV_W - 1) * bp, D_CONF), *common,
                                nb=bp, seq=lp, row_block0=0, name="conf_seq_prompt")
            ds, cs = _conf_call(u, _time_major(state_conf_conv[j]), *common,
                                nb=bs, seq=ls, row_block0=npt_m, name="conf_seq_sample")
            new_cconv.append((_batch_major(cp, bp), _batch_major(cs, bs)))
            mixed, w_out, b_out = (dp, ds), conf_w_pw2[j], conf_b_pw2[j]

        x, hf = _proj_ln_call(*mixed, w_out.astype(BF16), b_out, x, g_m, ln_mix_g[i], ln_mix_b[i], sc_f, sh_f,
                              npt_m, "mix_out_ln")
        idx, gate = _retrieve_call(hf, peer_w_q[i].astype(BF16),
                                   peer_keys[i].astype(BF16).reshape(2 * N_PEER_HEADS, N_KEYS, HALF_KEY))
        act = _peer_u_call(idx, hf, _pack_table(peer_u, i))
        x = _peer_v_call(idx, act, gate, _pack_table(peer_v, i), x, g_f, ln_ffn_g[i], ln_ffn_b[i], npt_t)

    y_prompt = _batch_major(x[:tp], bp)
    y_sample = _batch_major(x[tp:], bs)
    pick = lambda pairs, k: jnp.stack([p[k] for p in pairs])
    return (y_prompt, y_sample, pick(new_h, 0), pick(new_lconv, 0), pick(new_cconv, 0),
            pick(new_h, 1), pick(new_lconv, 1), pick(new_cconv, 1))
```

```python
import functools
import math

import jax
import jax.numpy as jnp
from jax import lax
from jax.experimental import pallas as pl
from jax.experimental.pallas import tpu as pltpu

F32 = jnp.float32
BF16 = jnp.bfloat16

D_MODEL = 1024
DEPTH = 2
D_RNN = 1280
N_RNN_HEADS = 10
RNN_HEAD_DIM = 128
LRU_CONV_W = 4
LRU_C = 8.0
D_CONF = 1024
CONF_CONV_W = 31
N_PEER_HEADS = 8
N_KEYS = 128
N_EXPERTS = N_KEYS * N_KEYS
HALF_KEY = 128
TOPK = 16
N_SEL = N_PEER_HEADS * TOPK
DEEPNORM_ALPHA = (2 * DEPTH) ** 0.25
LN_EPS = 1e-5

LANES = 128
GROUP = 128
TM = 512
TT = 128
WORDS = D_MODEL // 2
SLABS = WORDS // LANES
N_STAGE = 4
VMEM_LIMIT = 56 * 1024 * 1024

_NT = (((1,), (1,)), ((), ()))


def _cparams(n_axes, vmem=VMEM_LIMIT):
    return pltpu.CompilerParams(dimension_semantics=("arbitrary",) * n_axes, vmem_limit_bytes=vmem)


def _gelu(x):
    return 0.5 * x * (1.0 + lax.erf(x * (1.0 / math.sqrt(2.0))))


def _affine(x, scale, shift=None):
    rows, c = x.shape
    xr = x.reshape(rows // GROUP, GROUP, c) * scale[None]
    if shift is not None:
        xr = xr + shift[None]
    return xr.reshape(rows, c)


def _layer_norm(z, g, b):
    mu = jnp.mean(z, axis=-1, keepdims=True)
    zc = z - mu
    var = jnp.mean(zc * zc, axis=-1, keepdims=True)
    return zc * lax.rsqrt(var + LN_EPS) * g + b


def _mod_kernel(c_ref, w_ref, b_ref, o_ref):
    c = c_ref[...]
    a = (c * jax.nn.sigmoid(c)).astype(BF16)
    o_ref[0] = jnp.dot(a, w_ref[0].astype(BF16), preferred_element_type=F32) + b_ref[0]


def _mod_call(c_all, w_ada, b_ada):
    nb = c_all.shape[0]
    tn = 1536
    return pl.pallas_call(
        _mod_kernel,
        grid=(DEPTH, 6 * D_MODEL // tn),
        in_specs=[
            pl.BlockSpec((nb, D_MODEL), lambda l, j: (0, 0)),
            pl.BlockSpec((1, D_MODEL, tn), lambda l, j: (l, 0, j)),
            pl.BlockSpec((1, 1, tn), lambda l, j: (l, 0, j)),
        ],
        out_specs=pl.BlockSpec((1, nb, tn), lambda l, j: (l, 0, j)),
        out_shape=jax.ShapeDtypeStruct((DEPTH, nb, 6 * D_MODEL), F32),
        compiler_params=_cparams(2),
        name="adaln_mod",
    )(c_all, w_ada, b_ada.reshape(DEPTH, 1, 6 * D_MODEL))


def _mm_mod_kernel(x_ref, sc_ref, sh_ref, w_ref, *rest, glu):
    o_ref = rest[-1]
    xm = _affine(x_ref[...], 1.0 + sc_ref[0], sh_ref[0])
    p = jnp.dot(xm.astype(BF16), w_ref[...], preferred_element_type=F32)
    if len(rest) == 2:
        p = p + rest[0][...]
    if glu:
        n = p.shape[-1] // 2
        p = p[:, :n] * jax.nn.sigmoid(p[:, n:])
    o_ref[...] = p


def _mm_mod_call(x, sc, sh, w_bf16, bias, n_prompt_tiles, glu, name):
    t = x.shape[0]
    n = w_bf16.shape[1]
    n_out = n // 2 if glu else n
    sel = lambda i: (jnp.where(i >= n_prompt_tiles, 1, 0), 0, 0)
    in_specs = [
        pl.BlockSpec((TM, D_MODEL), lambda i: (i, 0)),
        pl.BlockSpec((1, GROUP, D_MODEL), sel),
        pl.BlockSpec((1, GROUP, D_MODEL), sel),
        pl.BlockSpec((D_MODEL, n), lambda i: (0, 0)),
    ]
    args = [x, sc, sh, w_bf16]
    if bias is not None:
        in_specs.append(pl.BlockSpec((1, n), lambda i: (0, 0)))
        args.append(bias.reshape(1, n))
    return pl.pallas_call(
        functools.partial(_mm_mod_kernel, glu=glu),
        grid=(t // TM,),
        in_specs=in_specs,
        out_specs=pl.BlockSpec((TM, n_out), lambda i: (i, 0)),
        out_shape=jax.ShapeDtypeStruct((t, n_out), F32),
        compiler_params=_cparams(1),
        name=name,
    )(*args)


def _lru_kernel(proj_ref, conv0_ref, h0_ref, cw_ref, cb_ref, wa_ref, ba_ref, wi_ref, bi_ref, lam_ref,
                yg_ref, hnew_ref, convnew_ref, win_ref, xc_ref, a_ref, b_ref, h_ref, *, nb, tl, cw, n_tiles):
    rows = tl * nb
    halo = (LRU_CONV_W - 1) * nb

    @pl.when(pl.program_id(0) == 0)
    def _():
        win_ref[0:halo, :] = conv0_ref[...]
        h_ref[...] = h0_ref[...]

    win_ref[halo:halo + rows, :] = proj_ref[:, :D_RNN]
    xc = cb_ref[...] + win_ref[0:rows, :] * cw_ref[0:1, :]
    for w in range(1, LRU_CONV_W):
        xc = xc + win_ref[w * nb:w * nb + rows, :] * cw_ref[w:w + 1, :]
    xc_ref[...] = xc
    carry = win_ref[rows:rows + halo, :]
    convnew_ref[...] = carry
    if n_tiles > 1:
        win_ref[0:halo, :] = carry

    lam = lam_ref[...]
    neg = -lam
    softplus = jnp.maximum(neg, 0.0) + jnp.log1p(jnp.exp(-jnp.abs(neg)))
    for hd in range(N_RNN_HEADS):
        cs = slice(hd * RNN_HEAD_DIM, (hd + 1) * RNN_HEAD_DIM)
        xh = xc_ref[:, cs]
        xb = xh.astype(BF16)
        r = jax.nn.sigmoid(jnp.dot(xb, wa_ref[hd], preferred_element_type=F32) + ba_ref[:, cs])
        ig = jax.nn.sigmoid(jnp.dot(xb, wi_ref[hd], preferred_element_type=F32) + bi_ref[:, cs])
        log_a = (-LRU_C * r) * softplus[:, cs]
        th = jnp.tanh(log_a)
        a_ref[:, cs] = jnp.exp(log_a)
        b_ref[:, cs] = jnp.sqrt(-2.0 * th / (1.0 - th)) * (ig * xh)

    for c0 in range(0, D_RNN, cw):
        cs = slice(c0, c0 + cw)

        def step(l, h, cs=cs):
            rs = pl.ds(pl.multiple_of(l * nb, nb), nb)
            h = a_ref[rs, cs] * h + b_ref[rs, cs]
            b_ref[rs, cs] = h
            return h

        h_ref[:, cs] = lax.fori_loop(0, tl, step, h_ref[:, cs])

    yg_ref[...] = b_ref[...] * _gelu(proj_ref[:, D_RNN:])
    hnew_ref[...] = h_ref[...]


def _lru_call(proj, conv0, h0, cw, cb, wa, ba, wi, bi, lam, *, nb, seq, row_block0, name):
    tl = TM // nb
    n_tiles = seq // tl
    halo = (LRU_CONV_W - 1) * nb
    chan = D_RNN if nb * D_RNN <= 16 * 1024 else LANES
    full = lambda shape: pl.BlockSpec(shape, lambda i: (0,) * len(shape))
    return pl.pallas_call(
        functools.partial(_lru_kernel, nb=nb, tl=tl, cw=chan, n_tiles=n_tiles),
        grid=(n_tiles,),
        in_specs=[
            pl.BlockSpec((TM, 2 * D_RNN), lambda i: (row_block0 + i, 0)),
            full((halo, D_RNN)),
            full((nb, D_RNN)),
            full((LRU_CONV_W, D_RNN)),
            full((1, D_RNN)),
            full((N_RNN_HEADS, RNN_HEAD_DIM, RNN_HEAD_DIM)),
            full((1, D_RNN)),
            full((N_RNN_HEADS, RNN_HEAD_DIM, RNN_HEAD_DIM)),
            full((1, D_RNN)),
            full((1, D_RNN)),
        ],
        out_specs=[
            pl.BlockSpec((TM, D_RNN), lambda i: (i, 0)),
            full((nb, D_RNN)),
            full((halo, D_RNN)),
        ],
        out_shape=[
            jax.ShapeDtypeStruct((seq * nb, D_RNN), F32),
            jax.ShapeDtypeStruct((nb, D_RNN), F32),
            jax.ShapeDtypeStruct((halo, D_RNN), F32),
        ],
        scratch_shapes=[
            pltpu.VMEM((halo + TM, D_RNN), F32),
            pltpu.VMEM((TM, D_RNN), F32),
            pltpu.VMEM((TM, D_RNN), F32),
            pltpu.VMEM((TM, D_RNN), F32),
            pltpu.VMEM((nb, D_RNN), F32),
        ],
        compiler_params=_cparams(1),
        name=name,
    )(proj, conv0, h0, cw, cb.reshape(1, D_RNN), wa, ba.reshape(1, D_RNN), wi, bi.reshape(1, D_RNN),
      lam.reshape(1, D_RNN))


def _conf_kernel(u_ref, conv0_ref, dw_ref, db_ref, g_ref, b_ref, o_ref, convnew_ref, win_ref, d_ref,
                 *, nb, tl, n_tiles):
    rows = tl * nb
    halo = (CONF_CONV_W - 1) * nb
    rc = 64
    cc = 256

    @pl.when(pl.program_id(0) == 0)
    def _():
        win_ref[0:halo, :] = conv0_ref[...]

    win_ref[halo:halo + rows, :] = u_ref[...]

    for c0 in range(0, D_CONF, cc):
        cs = slice(c0, c0 + cc)

        def chunk(r, carry, cs=cs):
            r0 = pl.multiple_of(r * rc, rc)
            acc = jnp.broadcast_to(db_ref[:, cs], (rc, cc))
            for w in range(CONF_CONV_W):
                acc = acc + win_ref[pl.ds(r0 + w * nb, rc), cs] * dw_ref[w:w + 1, cs]
            d_ref[pl.ds(r0, rc), cs] = acc
            return carry

        lax.fori_loop(0, rows // rc, chunk, 0)

    convnew_ref[...] = win_ref[rows:rows + halo, :]
    if n_tiles > 1:
        win_ref[0:halo, :] = win_ref[rows:rows + halo, :]

    y = _layer_norm(d_ref[...], g_ref[...], b_ref[...])
    o_ref[...] = y * jax.nn.sigmoid(y)


def _conf_call(u, conv0, dw, db, g, b, *, nb, seq, row_block0, name):
    tl = TM // nb
    n_tiles = seq // tl
    halo = (CONF_CONV_W - 1) * nb
    full = lambda shape: pl.BlockSpec(shape, lambda i: (0,) * len(shape))
    return pl.pallas_call(
        functools.partial(_conf_kernel, nb=nb, tl=tl, n_tiles=n_tiles),
        grid=(n_tiles,),
        in_specs=[
            pl.BlockSpec((TM, D_CONF), lambda i: (row_block0 + i, 0)),
            full((halo, D_CONF)),
            full((CONF_CONV_W, D_CONF)),
            full((1, D_CONF)),
            full((1, D_CONF)),
            full((1, D_CONF)),
        ],
        out_specs=[
            pl.BlockSpec((TM, D_CONF), lambda i: (i, 0)),
            full((halo, D_CONF)),
        ],
        out_shape=[
            jax.ShapeDtypeStruct((seq * nb, D_CONF), F32),
            jax.ShapeDtypeStruct((halo, D_CONF), F32),
        ],
        scratch_shapes=[
            pltpu.VMEM((halo + TM, D_CONF), F32),
            pltpu.VMEM((TM, D_CONF), F32),
        ],
        compiler_params=_cparams(1),
        name=name,
    )(u, conv0, dw, db.reshape(1, D_CONF), g.reshape(1, D_CONF), b.reshape(1, D_CONF))


def _proj_ln_kernel(yp_ref, ys_ref, w_ref, x_ref, gm_ref, lng_ref, lnb_ref, scf_ref, shf_ref, *rest, n_prompt_tiles):
    x1_ref, hf_ref = rest[-2:]
    y = jnp.where(pl.program_id(0) < n_prompt_tiles, yp_ref[...], ys_ref[...])
    out = jnp.dot(y.astype(BF16), w_ref[...], preferred_element_type=F32)
    if len(rest) == 3:
        out = out + rest[0][...]
    z = DEEPNORM_ALPHA * x_ref[...] + _affine(out, gm_ref[0])
    x1 = _layer_norm(z, lng_ref[...], lnb_ref[...])
    x1_ref[...] = x1
    hf_ref[...] = _affine(x1, 1.0 + scf_ref[0], shf_ref[0])


def _proj_ln_call(y_prompt, y_sample, w_bf16, bias, x, gm, lng, lnb, scf, shf, n_prompt_tiles, name):
    t, k = x.shape[0], y_prompt.shape[1]
    sel = lambda i: (jnp.where(i >= n_prompt_tiles, 1, 0), 0, 0)
    row = pl.BlockSpec((1, D_MODEL), lambda i: (0, 0))
    mod = pl.BlockSpec((1, GROUP, D_MODEL), sel)
    tile = pl.BlockSpec((TM, D_MODEL), lambda i: (i, 0))
    in_specs = [
        pl.BlockSpec((TM, k), lambda i: (jnp.minimum(i, n_prompt_tiles - 1), 0)),
        pl.BlockSpec((TM, k), lambda i: (jnp.maximum(i - n_prompt_tiles, 0), 0)),
        pl.BlockSpec((k, D_MODEL), lambda i: (0, 0)),
        tile, mod, row, row, mod, mod,
    ]
    args = [y_prompt, y_sample, w_bf16, x, gm, lng.reshape(1, D_MODEL), lnb.reshape(1, D_MODEL), scf, shf]
    if bias is not None:
        in_specs.append(row)
        args.append(bias.reshape(1, D_MODEL))
    return pl.pallas_call(
        functools.partial(_proj_ln_kernel, n_prompt_tiles=n_prompt_tiles),
        grid=(t // TM,),
        in_specs=in_specs,
        out_specs=[tile, tile],
        out_shape=[jax.ShapeDtypeStruct((t, D_MODEL), F32)] * 2,
        compiler_params=_cparams(1),
        name=name,
    )(*args)


SUB = 8

_CELLS = [(i, j) for i in range(TOPK) for j in range(TOPK) if (i + 1) * (j + 1) <= TOPK]
_CELLS = _CELLS + [None] * (-len(_CELLS) % SUB)
_CELL_GROUPS = [_CELLS[k:k + SUB] for k in range(0, len(_CELLS), SUB)]


def _stack_rows(rows, r_iota):
    out = jnp.broadcast_to(rows[-1], r_iota.shape)
    for r in range(SUB - 2, -1, -1):
        if rows[r] is not rows[r + 1]:
            out = jnp.where(r_iota <= r, rows[r], out)
    return out


def _row_iota(shape):
    return lax.broadcasted_iota(jnp.int32, shape, 0).astype(F32)


def _sort_network(lo, hi):
    def merge(lo, hi, r):
        step = 2 * r
        if step < hi - lo:
            yield from merge(lo, hi, step)
            yield from merge(lo + r, hi, step)
            yield from ((i, i + r) for i in range(lo + r, hi - r, step))
        else:
            yield (lo, lo + r)

    if hi > lo:
        mid = lo + (hi - lo) // 2
        yield from _sort_network(lo, mid)
        yield from _sort_network(mid + 1, hi)
        yield from merge(lo, hi, 1)


def _top_rows(s, k):
    n = s.shape[0]
    depth = n // SUB
    iota = _row_iota(s.shape)
    neg = [-s[v * SUB:(v + 1) * SUB, :] for v in range(depth)]
    idx = [iota[v * SUB:(v + 1) * SUB, :] for v in range(depth)]
    for a, b in _sort_network(0, depth - 1):
        tie = neg[a] == neg[b]
        swap = jnp.where(tie, idx[b], neg[b]) < jnp.where(tie, idx[a], neg[a])
        neg[a], neg[b] = jnp.where(swap, neg[b], neg[a]), jnp.where(swap, neg[a], neg[b])
        idx[a], idx[b] = jnp.where(swap, idx[b], idx[a]), jnp.where(swap, idx[a], idx[b])
    vals, idxs = [], []
    for r in range(k):
        m = jnp.min(neg[0], axis=0, keepdims=True)
        am = jnp.min(jnp.where(neg[0] == m, idx[0], float(n)), axis=0, keepdims=True)
        vals.append(-m)
        idxs.append(am)
        won = idx[0] == am
        for d in range(min(depth - 1, k - 1 - r)):
            neg[d] = jnp.where(won, neg[d + 1], neg[d])
            idx[d] = jnp.where(won, idx[d + 1], idx[d])
    return vals, idxs


def _retrieve_kernel(hf_ref, wq_ref, keys_ref, idx_ref, g_ref, g_t_ref):
    q = jnp.dot(hf_ref[...].astype(BF16), wq_ref[...], preferred_element_type=F32).astype(BF16)
    r_iota = lax.broadcasted_iota(jnp.int32, (SUB, TT), 0)
    neg_inf = jnp.full((1, TT), -jnp.inf, F32)
    zero = jnp.zeros((1, TT), F32)
    for h in range(N_PEER_HEADS):
        halves = []
        for p in range(2):
            hp = h * 2 + p
            s = lax.dot_general(keys_ref[hp], q[:, hp * HALF_KEY:(hp + 1) * HALF_KEY], _NT,
                                preferred_element_type=F32)
            halves.append(_top_rows(s, TOPK))
        (v1, i1), (v2, i2) = halves
        i1 = [iv * float(N_KEYS) for iv in i1]
        cand, cidx = [], []
        for cells in _CELL_GROUPS:
            pick = lambda rows, pos, pad: _stack_rows([pad if c is None else rows[c[pos]] for c in cells], r_iota)
            cand.append(pick(v1, 0, neg_inf) + pick(v2, 1, neg_inf))
            cidx.append(pick(i1, 0, zero) + pick(i2, 1, zero))
        cand = jnp.concatenate(cand, axis=0)
        cidx = jnp.concatenate(cidx, axis=0)
        iota = _row_iota(cand.shape)
        top_s, experts = [], []
        for _ in range(TOPK):
            m = jnp.max(cand, axis=0, keepdims=True)
            am = jnp.min(jnp.where(cand == m, iota, float(len(_CELLS))), axis=0, keepdims=True)
            hit = iota == am
            experts.append(jnp.max(jnp.where(hit, cidx, -1.0), axis=0, keepdims=True))
            top_s.append(m)
            cand = jnp.where(hit, -jnp.inf, cand)
        e = [jnp.exp(_stack_rows(top_s[k:k + SUB], r_iota) - top_s[0]) for k in range(0, TOPK, SUB)]
        denom = jnp.sum(sum(e[1:], e[0]), axis=0, keepdims=True)
        for k, ev in enumerate(e):
            rows = slice(h * TOPK + k * SUB, h * TOPK + (k + 1) * SUB)
            g_t_ref[rows, :] = ev / denom
            ids = _stack_rows(experts[k * SUB:(k + 1) * SUB], r_iota)
            idx_ref[rows, :] = ids.astype(jnp.int32) * SLABS
    g_ref[...] = g_t_ref[...].T


def _retrieve_call(hf, wq_bf16, keys_bf16):
    t = hf.shape[0]
    nq = wq_bf16.shape[1]
    return pl.pallas_call(
        _retrieve_kernel,
        grid=(t // TT,),
        in_specs=[
            pl.BlockSpec((TT, D_MODEL), lambda i: (i, 0)),
            pl.BlockSpec((D_MODEL, nq), lambda i: (0, 0)),
            pl.BlockSpec((2 * N_PEER_HEADS, N_KEYS, HALF_KEY), lambda i: (0, 0, 0)),
        ],
        out_specs=[
            pl.BlockSpec((N_SEL, TT), lambda i: (0, i)),
            pl.BlockSpec((TT, N_SEL), lambda i: (i, 0)),
        ],
        out_shape=[
            jax.ShapeDtypeStruct((N_SEL, t), jnp.int32),
            jax.ShapeDtypeStruct((t, N_SEL), F32),
        ],
        scratch_shapes=[pltpu.VMEM((N_SEL, TT), F32)],
        compiler_params=_cparams(1),
        name="peer_retrieve",
    )(hf, wq_bf16, keys_bf16)


def _pack_kernel(t_ref, o_ref):
    x = t_ref[0]
    w = pltpu.pack_elementwise([x[:, :WORDS], x[:, WORDS:]], packed_dtype=BF16)
    for j in range(SLABS):
        o_ref[pl.ds(j, TM, stride=SLABS), :] = w[:, j * LANES:(j + 1) * LANES]


def _pack_table(tabs, layer):
    return pl.pallas_call(
        _pack_kernel,
        grid=(N_EXPERTS // TM,),
        in_specs=[pl.BlockSpec((1, TM, D_MODEL), lambda i: (layer, i, 0))],
        out_specs=pl.BlockSpec((TM * SLABS, LANES), lambda i: (i, 0)),
        out_shape=jax.ShapeDtypeStruct((N_EXPERTS * SLABS, LANES), jnp.uint32),
        compiler_params=_cparams(1),
        name="peer_pack_table",
    )(tabs)


def _gather_token(idx_ref, t, tab_ref, stage_ref):
    for k in range(N_SEL):
        i = pl.multiple_of(idx_ref.at[k][t], SLABS)
        stage_ref[pl.ds(k * SLABS, SLABS), :] = tab_ref[pl.ds(i, SLABS), :]


def _unpack_words(w, half):
    return pltpu.unpack_elementwise(w, index=half, packed_dtype=BF16, unpacked_dtype=F32)


def _unpack_stage(stage_ref):
    his, los = [], []
    for j in range(SLABS):
        w = stage_ref[pl.ds(j, N_SEL, stride=SLABS), :]
        his.append(_unpack_words(w, 0).astype(BF16))
        los.append(_unpack_words(w, 1).astype(BF16))
    return jnp.concatenate(his + los, axis=1)


def _token_pipeline(idx_hbm, bufs, sems, tab_ref, stages, produce, consume):
    step, n_steps = pl.program_id(0), pl.num_programs(0)
    n_groups = TT // N_STAGE

    def copy(tile, slot):
        cols = pl.ds(pl.multiple_of(tile * TT, TT), TT)
        return pltpu.make_async_copy(idx_hbm.at[:, cols], bufs[slot], sems.at[slot])

    def gather_group(idx_ref, t0):
        for q in range(N_STAGE):
            _gather_token(idx_ref, t0 + q, tab_ref, stages[q])

    def group(t0, refill):
        results = [produce(t0 + q, stages[q]) for q in range(N_STAGE)]
        refill()
        consume(t0, results)

    @pl.when(step == 0)
    def _():
        copy(0, 0).start()
        copy(0, 0).wait()
        gather_group(bufs[0], 0)

    for slot in range(2):
        @pl.when(lax.rem(step, 2) == slot)
        def _(slot=slot):
            has_next = step + 1 < n_steps

            @pl.when(has_next)
            def _():
                copy(step + 1, 1 - slot).start()

            def body(i, carry):
                t0 = i * N_STAGE
                group(t0, lambda: gather_group(bufs[slot], t0 + N_STAGE))
                return carry

            lax.fori_loop(0, n_groups - 1, body, 0)

            @pl.when(has_next)
            def _():
                copy(step + 1, 1 - slot).wait()
                group(TT - N_STAGE, lambda: gather_group(bufs[1 - slot], 0))

            @pl.when(jnp.logical_not(has_next))
            def _():
                group(TT - N_STAGE, lambda: None)


def _peer_u_kernel(idx_hbm, x_ref, tab_ref, act_ref, xb_ref, act_t_ref, idx0, idx1, sems, *stages):
    xb_ref[...] = x_ref[...].astype(BF16).astype(F32)
    act_t_ref[...] = jnp.zeros((N_SEL, TT), F32)
    lane = lax.broadcasted_iota(jnp.int32, (N_SEL, TT), 1)

    def produce(t, stage):
        xr = xb_ref[pl.ds(t, 1), :]
        p = None
        for j in range(SLABS):
            w = stage[pl.ds(j, N_SEL, stride=SLABS), :]
            term = (_unpack_words(w, 0) * xr[:, j * LANES:(j + 1) * LANES]
                    + _unpack_words(w, 1) * xr[:, WORDS + j * LANES:WORDS + (j + 1) * LANES])
            p = term if p is None else p + term
        return jnp.sum(p, axis=1, keepdims=True)

    def consume(t0, cols):
        acc = act_t_ref[...]
        for q, r in enumerate(cols):
            acc = jnp.where(lane == t0 + q, r, acc)
        act_t_ref[...] = acc

    _token_pipeline(idx_hbm, (idx0, idx1), sems, tab_ref, stages, produce, consume)
    act_ref[...] = act_t_ref[...].T


_INDEX_SCRATCH = [pltpu.SMEM((N_SEL, TT), jnp.int32), pltpu.SMEM((N_SEL, TT), jnp.int32),
                  pltpu.SemaphoreType.DMA((2,))]
_STAGE_SCRATCH = [pltpu.VMEM((SLABS * N_SEL, LANES), jnp.uint32)] * N_STAGE


def _peer_u_call(idx, hf, tab):
    t = hf.shape[0]
    return pl.pallas_call(
        _peer_u_kernel,
        grid=(t // TT,),
        in_specs=[
            pl.BlockSpec(memory_space=pl.ANY),
            pl.BlockSpec((TT, D_MODEL), lambda i: (i, 0)),
            pl.BlockSpec(memory_space=pltpu.VMEM),
        ],
        out_specs=pl.BlockSpec((TT, N_SEL), lambda i: (i, 0)),
        out_shape=jax.ShapeDtypeStruct((t, N_SEL), F32),
        scratch_shapes=[pltpu.VMEM((TT, D_MODEL), F32), pltpu.VMEM((N_SEL, TT), F32)]
        + _INDEX_SCRATCH + _STAGE_SCRATCH,
        compiler_params=_cparams(1),
        name="peer_u",
    )(idx, hf, tab)


def _peer_v_kernel(idx_hbm, act_ref, g_ref, tab_ref, x_ref, gf_ref, lng_ref, lnb_ref, o_ref, w_ref, f_ref,
                   idx0, idx1, sems, *stages):
    w_ref[...] = g_ref[...] * _gelu(act_ref[...])

    def produce(t, stage):
        wr = jnp.broadcast_to(w_ref[pl.ds(t, 1), :], (8, N_SEL)).astype(BF16)
        return jnp.dot(wr, _unpack_stage(stage), preferred_element_type=F32)

    def consume(t0, rows):
        for q, f in enumerate(rows):
            f_ref[pl.ds(t0 + q, 1), :] = f[0:1]

    _token_pipeline(idx_hbm, (idx0, idx1), sems, tab_ref, stages, produce, consume)
    z = DEEPNORM_ALPHA * x_ref[...] + gf_ref[0] * f_ref[...]
    o_ref[...] = _layer_norm(z, lng_ref[...], lnb_ref[...])


def _peer_v_call(idx, act, g, tab, x, gf, lng, lnb, n_prompt_tiles):
    t = x.shape[0]
    sel = lambda i: (jnp.where(i >= n_prompt_tiles, 1, 0), 0, 0)
    row = pl.BlockSpec((1, D_MODEL), lambda i: (0, 0))
    return pl.pallas_call(
        _peer_v_kernel,
        grid=(t // TT,),
        in_specs=[
            pl.BlockSpec(memory_space=pl.ANY),
            pl.BlockSpec((TT, N_SEL), lambda i: (i, 0)),
            pl.BlockSpec((TT, N_SEL), lambda i: (i, 0)),
            pl.BlockSpec(memory_space=pltpu.VMEM),
            pl.BlockSpec((TT, D_MODEL), lambda i: (i, 0)),
            pl.BlockSpec((1, GROUP, D_MODEL), sel),
            row, row,
        ],
        out_specs=pl.BlockSpec((TT, D_MODEL), lambda i: (i, 0)),
        out_shape=jax.ShapeDtypeStruct((t, D_MODEL), F32),
        scratch_shapes=[pltpu.VMEM((TT, N_SEL), F32), pltpu.VMEM((TT, D_MODEL), F32)]
        + _INDEX_SCRATCH + _STAGE_SCRATCH,
        compiler_params=_cparams(1),
        name="peer_v",
    )(idx, act, g, tab, x, gf, lng.reshape(1, D_MODEL), lnb.reshape(1, D_MODEL))


def _time_major(a):
    return jnp.swapaxes(a, 0, 1).reshape(a.shape[0] * a.shape[1], a.shape[2])


def _batch_major(a, nb):
    return jnp.swapaxes(a.reshape(a.shape[0] // nb, nb, a.shape[1]), 0, 1)


def kernel(x_prompt, x_sample, state_lru_h, state_lru_conv, state_conf_conv, c_prompt, c_sample, w_ada, b_ada, ln_mix_g, ln_mix_b, ln_ffn_g, ln_ffn_b, lru_w_in, lru_conv_w, lru_conv_b, lru_w_a, lru_b_a, lru_w_i, lru_b_i, lru_lambda, lru_w_out, conf_w_pw1, conf_b_pw1, conf_dw_w, conf_dw_b, conf_ln_g, conf_ln_b, conf_w_pw2, conf_b_pw2, peer_w_q, peer_keys, peer_u, peer_v):
    bp, lp, _ = x_prompt.shape
    bs, ls, _ = x_sample.shape
    tp, ts = bp * lp, bs * ls
    assert GROUP % bp == 0 and bs == GROUP and TM % GROUP == 0
    for nb, seq in ((bp, lp), (bs, ls)):
        assert (seq * nb) % TM == 0 and (seq * nb == TM or TM // nb >= CONF_CONV_W - 1)
    npt_m, npt_t = tp // TM, tp // TT

    x = jnp.concatenate([_time_major(x_prompt), _time_major(x_sample)], axis=0)

    mod = _mod_call(jnp.concatenate([c_prompt, c_sample], axis=0), w_ada, b_ada)
    mod = mod.reshape(DEPTH, bp + bs, 6, D_MODEL)
    pat = jnp.stack([jnp.tile(mod[:, :bp], (1, GROUP // bp, 1, 1)), mod[:, bp:]], axis=1)
    pat = jnp.transpose(pat, (0, 3, 1, 2, 4))
    mods = [[pat[i, k] for k in range(6)] for i in range(DEPTH)]

    zeros = lambda *s: jnp.zeros(s, F32)
    new_h, new_lconv, new_cconv = [], [], []
    for i in range(DEPTH):
        sh_m, sc_m, g_m, sh_f, sc_f, g_f = mods[i]
        j = i // 2
        if i % 2 == 0:
            proj = _mm_mod_call(x, sc_m, sh_m, lru_w_in[j].astype(BF16), None, npt_m, False, "lru_in_proj")
            wa, wi = lru_w_a[j].astype(BF16), lru_w_i[j].astype(BF16)
            common = (lru_conv_w[j], lru_conv_b[j], wa, lru_b_a[j], wi, lru_b_i[j], lru_lambda[j])
            yp, hp, cp = _lru_call(proj, zeros((LRU_CONV_W - 1) * bp, D_RNN), zeros(bp, D_RNN), *common,
                                   nb=bp, seq=lp, row_block0=0, name="lru_seq_prompt")
            ys, hs, cs = _lru_call(proj, _time_major(state_lru_conv[j]), state_lru_h[j], *common,
                                   nb=bs, seq=ls, row_block0=npt_m, name="lru_seq_sample")
            new_h.append((hp, hs))
            new_lconv.append((_batch_major(cp, bp), _batch_major(cs, bs)))
            mixed, w_out, b_out = (yp, ys), lru_w_out[j], None
        else:
            u = _mm_mod_call(x, sc_m, sh_m, conf_w_pw1[j].astype(BF16), conf_b_pw1[j], npt_m, True, "conf_pw1_glu")
            common = (conf_dw_w[j], conf_dw_b[j], conf_ln_g[j], conf_ln_b[j])
            dp, cp = _conf_call(u, zeros((CONF_CONV_W - 1) * bp, D_CONF), *common,
                                nb=bp, seq=lp, row_block0=0, name="conf_seq_prompt")
            ds, cs = _conf_call(u, _time_major(state_conf_conv[j]), *common,
                                nb=bs, seq=ls, row_block0=npt_m, name="conf_seq_sample")
            new_cconv.append((_batch_major(cp, bp), _batch_major(cs, bs)))
            mixed, w_out, b_out = (dp, ds), conf_w_pw2[j], conf_b_pw2[j]

        x, hf = _proj_ln_call(*mixed, w_out.astype(BF16), b_out, x, g_m, ln_mix_g[i], ln_mix_b[i], sc_f, sh_f,
                              npt_m, "mix_out_ln")
        idx, gate = _retrieve_call(hf, peer_w_q[i].astype(BF16),
                                   peer_keys[i].astype(BF16).reshape(2 * N_PEER_HEADS, N_KEYS, HALF_KEY))
        act = _peer_u_call(idx, hf, _pack_table(peer_u, i))
        x = _peer_v_call(idx, act, gate, _pack_table(peer_v, i), x, g_f, ln_ffn_g[i], ln_ffn_b[i], npt_t)

    y_prompt = _batch_major(x[:tp], bp)
    y_sample = _batch_major(x[tp:], bs)
    pick = lambda pairs, k: jnp.stack([p[k] for p in pairs])
    return (y_prompt, y_sample, pick(new_h, 0), pick(new_lconv, 0), pick(new_cconv, 0),
            pick(new_h, 1), pick(new_lconv, 1), pick(new_cconv, 1))
```

```python
import functools
import math

import jax
import jax.numpy as jnp
from jax import lax
from jax.experimental import pallas as pl
from jax.experimental.pallas import tpu as pltpu

F32 = jnp.float32
BF16 = jnp.bfloat16

D_MODEL = 1024
DEPTH = 2
D_RNN = 1280
N_RNN_HEADS = 10
RNN_HEAD_DIM = 128
LRU_CONV_W = 4
LRU_C = 8.0
D_CONF = 1024
CONF_CONV_W = 31
N_PEER_HEADS = 8
N_KEYS = 128
N_EXPERTS = N_KEYS * N_KEYS
HALF_KEY = 128
TOPK = 16
N_SEL = N_PEER_HEADS * TOPK
DEEPNORM_ALPHA = (2 * DEPTH) ** 0.25
LN_EPS = 1e-5

LANES = 128
GROUP = 128
TM = 512
TT = 128
WORDS = D_MODEL // 2
SLABS = WORDS // LANES
N_STAGE = 4
VMEM_LIMIT = 56 * 1024 * 1024

_NT = (((1,), (1,)), ((), ()))


def _cparams(n_axes, vmem=VMEM_LIMIT):
    return pltpu.CompilerParams(dimension_semantics=("arbitrary",) * n_axes, vmem_limit_bytes=vmem)


def _gelu(x):
    return 0.5 * x * (1.0 + lax.erf(x * (1.0 / math.sqrt(2.0))))


def _affine(x, scale, shift=None):
    rows, c = x.shape
    xr = x.reshape(rows // GROUP, GROUP, c) * scale[None]
    if shift is not None:
        xr = xr + shift[None]
    return xr.reshape(rows, c)


def _layer_norm(z, g, b):
    mu = jnp.mean(z, axis=-1, keepdims=True)
    zc = z - mu
    var = jnp.mean(zc * zc, axis=-1, keepdims=True)
    return zc * lax.rsqrt(var + LN_EPS) * g + b


def _mod_kernel(c_ref, w_ref, b_ref, o_ref):
    c = c_ref[...]
    a = (c * jax.nn.sigmoid(c)).astype(BF16)
    o_ref[0] = jnp.dot(a, w_ref[0].astype(BF16), preferred_element_type=F32) + b_ref[0]


def _mod_call(c_all, w_ada, b_ada):
    nb = c_all.shape[0]
    tn = 1536
    return pl.pallas_call(
        _mod_kernel,
        grid=(DEPTH, 6 * D_MODEL // tn),
        in_specs=[
            pl.BlockSpec((nb, D_MODEL), lambda l, j: (0, 0)),
            pl.BlockSpec((1, D_MODEL, tn), lambda l, j: (l, 0, j)),
            pl.BlockSpec((1, 1, tn), lambda l, j: (l, 0, j)),
        ],
        out_specs=pl.BlockSpec((1, nb, tn), lambda l, j: (l, 0, j)),
        out_shape=jax.ShapeDtypeStruct((DEPTH, nb, 6 * D_MODEL), F32),
        compiler_params=_cparams(2),
        name="adaln_mod",
    )(c_all, w_ada, b_ada.reshape(DEPTH, 1, 6 * D_MODEL))


def _mm_mod_kernel(x_ref, sc_ref, sh_ref, w_ref, *rest, glu):
    o_ref = rest[-1]
    xm = _affine(x_ref[...], 1.0 + sc_ref[0], sh_ref[0])
    p = jnp.dot(xm.astype(BF16), w_ref[...], preferred_element_type=F32)
    if len(rest) == 2:
        p = p + rest[0][...]
    if glu:
        n = p.shape[-1] // 2
        p = p[:, :n] * jax.nn.sigmoid(p[:, n:])
    o_ref[...] = p


def _mm_mod_call(x, sc, sh, w_bf16, bias, n_prompt_tiles, glu, name):
    t = x.shape[0]
    n = w_bf16.shape[1]
    n_out = n // 2 if glu else n
    sel = lambda i: (jnp.where(i >= n_prompt_tiles, 1, 0), 0, 0)
    in_specs = [
        pl.BlockSpec((TM, D_MODEL), lambda i: (i, 0)),
        pl.BlockSpec((1, GROUP, D_MODEL), sel),
        pl.BlockSpec((1, GROUP, D_MODEL), sel),
        pl.BlockSpec((D_MODEL, n), lambda i: (0, 0)),
    ]
    args = [x, sc, sh, w_bf16]
    if bias is not None:
        in_specs.append(pl.BlockSpec((1, n), lambda i: (0, 0)))
        args.append(bias.reshape(1, n))
    return pl.pallas_call(
        functools.partial(_mm_mod_kernel, glu=glu),
        grid=(t // TM,),
        in_specs=in_specs,
        out_specs=pl.BlockSpec((TM, n_out), lambda i: (i, 0)),
        out_shape=jax.ShapeDtypeStruct((t, n_out), F32),
        compiler_params=_cparams(1),
        name=name,
    )(*args)


def _lru_kernel(proj_ref, conv0_ref, h0_ref, cw_ref, cb_ref, wa_ref, ba_ref, wi_ref, bi_ref, lam_ref,
                yg_ref, hnew_ref, convnew_ref, win_ref, xc_ref, a_ref, b_ref, h_ref, *, nb, tl, cw, n_tiles):
    rows = tl * nb
    halo = (LRU_CONV_W - 1) * nb

    @pl.when(pl.program_id(0) == 0)
    def _():
        win_ref[0:halo, :] = conv0_ref[...]
        h_ref[...] = h0_ref[...]

    win_ref[halo:halo + rows, :] = proj_ref[:, :D_RNN]
    xc = cb_ref[...] + win_ref[0:rows, :] * cw_ref[0:1, :]
    for w in range(1, LRU_CONV_W):
        xc = xc + win_ref[w * nb:w * nb + rows, :] * cw_ref[w:w + 1, :]
    xc_ref[...] = xc
    carry = win_ref[rows:rows + halo, :]
    convnew_ref[...] = carry
    if n_tiles > 1:
        win_ref[0:halo, :] = carry

    lam = lam_ref[...]
    neg = -lam
    softplus = jnp.maximum(neg, 0.0) + jnp.log1p(jnp.exp(-jnp.abs(neg)))
    for hd in range(N_RNN_HEADS):
        cs = slice(hd * RNN_HEAD_DIM, (hd + 1) * RNN_HEAD_DIM)
        xh = xc_ref[:, cs]
        xb = xh.astype(BF16)
        r = jax.nn.sigmoid(jnp.dot(xb, wa_ref[hd], preferred_element_type=F32) + ba_ref[:, cs])
        ig = jax.nn.sigmoid(jnp.dot(xb, wi_ref[hd], preferred_element_type=F32) + bi_ref[:, cs])
        log_a = (-LRU_C * r) * softplus[:, cs]
        th = jnp.tanh(log_a)
        a_ref[:, cs] = jnp.exp(log_a)
        b_ref[:, cs] = jnp.sqrt(-2.0 * th / (1.0 - th)) * (ig * xh)

    for c0 in range(0, D_RNN, cw):
        cs = slice(c0, c0 + cw)

        def step(l, h, cs=cs):
            rs = pl.ds(pl.multiple_of(l * nb, nb), nb)
            h = a_ref[rs, cs] * h + b_ref[rs, cs]
            b_ref[rs, cs] = h
            return h

        h_ref[:, cs] = lax.fori_loop(0, tl, step, h_ref[:, cs])

    yg_ref[...] = b_ref[...] * _gelu(proj_ref[:, D_RNN:])
    hnew_ref[...] = h_ref[...]


def _lru_call(proj, conv0, h0, cw, cb, wa, ba, wi, bi, lam, *, nb, seq, row_block0, name):
    tl = TM // nb
    n_tiles = seq // tl
    halo = (LRU_CONV_W - 1) * nb
    chan = D_RNN if nb * D_RNN <= 16 * 1024 else LANES
    full = lambda shape: pl.BlockSpec(shape, lambda i: (0,) * len(shape))
    return pl.pallas_call(
        functools.partial(_lru_kernel, nb=nb, tl=tl, cw=chan, n_tiles=n_tiles),
        grid=(n_tiles,),
        in_specs=[
            pl.BlockSpec((TM, 2 * D_RNN), lambda i: (row_block0 + i, 0)),
            full((halo, D_RNN)),
            full((nb, D_RNN)),
            full((LRU_CONV_W, D_RNN)),
            full((1, D_RNN)),
            full((N_RNN_HEADS, RNN_HEAD_DIM, RNN_HEAD_DIM)),
            full((1, D_RNN)),
            full((N_RNN_HEADS, RNN_HEAD_DIM, RNN_HEAD_DIM)),
            full((1, D_RNN)),
            full((1, D_RNN)),
        ],
        out_specs=[
            pl.BlockSpec((TM, D_RNN), lambda i: (i, 0)),
            full((nb, D_RNN)),
            full((halo, D_RNN)),
        ],
        out_shape=[
            jax.ShapeDtypeStruct((seq * nb, D_RNN), F32),
            jax.ShapeDtypeStruct((nb, D_RNN), F32),
            jax.ShapeDtypeStruct((halo, D_RNN), F32),
        ],
        scratch_shapes=[
            pltpu.VMEM((halo + TM, D_RNN), F32),
            pltpu.VMEM((TM, D_RNN), F32),
            pltpu.VMEM((TM, D_RNN), F32),
            pltpu.VMEM((TM, D_RNN), F32),
            pltpu.VMEM((nb, D_RNN), F32),
        ],
        compiler_params=_cparams(1),
        name=name,
    )(proj, conv0, h0, cw, cb.reshape(1, D_RNN), wa, ba.reshape(1, D_RNN), wi, bi.reshape(1, D_RNN),
      lam.reshape(1, D_RNN))


def _conf_kernel(u_ref, conv0_ref, dw_ref, db_ref, g_ref, b_ref, o_ref, convnew_ref, win_ref, d_ref,
                 *, nb, tl, n_tiles):
    rows = tl * nb
    halo = (CONF_CONV_W - 1) * nb
    rc = 64
    cc = 256

    @pl.when(pl.program_id(0) == 0)
    def _():
        win_ref[0:halo, :] = conv0_ref[...]

    win_ref[halo:halo + rows, :] = u_ref[...]

    for c0 in range(0, D_CONF, cc):
        cs = slice(c0, c0 + cc)

        def chunk(r, carry, cs=cs):
            r0 = pl.multiple_of(r * rc, rc)
            acc = jnp.broadcast_to(db_ref[:, cs], (rc, cc))
            for w in range(CONF_CONV_W):
                acc = acc + win_ref[pl.ds(r0 + w * nb, rc), cs] * dw_ref[w:w + 1, cs]
            d_ref[pl.ds(r0, rc), cs] = acc
            return carry

        lax.fori_loop(0, rows // rc, chunk, 0)

    convnew_ref[...] = win_ref[rows:rows + halo, :]
    if n_tiles > 1:
        win_ref[0:halo, :] = win_ref[rows:rows + halo, :]

    y = _layer_norm(d_ref[...], g_ref[...], b_ref[...])
    o_ref[...] = y * jax.nn.sigmoid(y)


def _conf_call(u, conv0, dw, db, g, b, *, nb, seq, row_block0, name):
    tl = TM // nb
    n_tiles = seq // tl
    halo = (CONF_CONV_W - 1) * nb
    full = lambda shape: pl.BlockSpec(shape, lambda i: (0,) * len(shape))
    return pl.pallas_call(
        functools.partial(_conf_kernel, nb=nb, tl=tl, n_tiles=n_tiles),
        grid=(n_tiles,),
        in_specs=[
            pl.BlockSpec((TM, D_CONF), lambda i: (row_block0 + i, 0)),
            full((halo, D_CONF)),
            full((CONF_CONV_W, D_CONF)),
            full((1, D_CONF)),
            full((1, D_CONF)),
            full((1, D_CONF)),
        ],
        out_specs=[
            pl.BlockSpec((TM, D_CONF), lambda i: (i, 0)),
            full((halo, D_CONF)),
        ],
        out_shape=[
            jax.ShapeDtypeStruct((seq * nb, D_CONF), F32),
            jax.ShapeDtypeStruct((halo, D_CONF), F32),
        ],
        scratch_shapes=[
            pltpu.VMEM((halo + TM, D_CONF), F32),
            pltpu.VMEM((TM, D_CONF), F32),
        ],
        compiler_params=_cparams(1),
        name=name,
    )(u, conv0, dw, db.reshape(1, D_CONF), g.reshape(1, D_CONF), b.reshape(1, D_CONF))


def _proj_ln_kernel(yp_ref, ys_ref, w_ref, x_ref, gm_ref, lng_ref, lnb_ref, scf_ref, shf_ref, *rest, n_prompt_tiles):
    x1_ref, hf_ref = rest[-2:]
    y = jnp.where(pl.program_id(0) < n_prompt_tiles, yp_ref[...], ys_ref[...])
    out = jnp.dot(y.astype(BF16), w_ref[...], preferred_element_type=F32)
    if len(rest) == 3:
        out = out + rest[0][...]
    z = DEEPNORM_ALPHA * x_ref[...] + _affine(out, gm_ref[0])
    x1 = _layer_norm(z, lng_ref[...], lnb_ref[...])
    x1_ref[...] = x1
    hf_ref[...] = _affine(x1, 1.0 + scf_ref[0], shf_ref[0])


def _proj_ln_call(y_prompt, y_sample, w_bf16, bias, x, gm, lng, lnb, scf, shf, n_prompt_tiles, name):
    t, k = x.shape[0], y_prompt.shape[1]
    sel = lambda i: (jnp.where(i >= n_prompt_tiles, 1, 0), 0, 0)
    row = pl.BlockSpec((1, D_MODEL), lambda i: (0, 0))
    mod = pl.BlockSpec((1, GROUP, D_MODEL), sel)
    tile = pl.BlockSpec((TM, D_MODEL), lambda i: (i, 0))
    in_specs = [
        pl.BlockSpec((TM, k), lambda i: (jnp.minimum(i, n_prompt_tiles - 1), 0)),
        pl.BlockSpec((TM, k), lambda i: (jnp.maximum(i - n_prompt_tiles, 0), 0)),
        pl.BlockSpec((k, D_MODEL), lambda i: (0, 0)),
        tile, mod, row, row, mod, mod,
    ]
    args = [y_prompt, y_sample, w_bf16, x, gm, lng.reshape(1, D_MODEL), lnb.reshape(1, D_MODEL), scf, shf]
    if bias is not None:
        in_specs.append(row)
        args.append(bias.reshape(1, D_MODEL))
    return pl.pallas_call(
        functools.partial(_proj_ln_kernel, n_prompt_tiles=n_prompt_tiles),
        grid=(t // TM,),
        in_specs=in_specs,
        out_specs=[tile, tile],
        out_shape=[jax.ShapeDtypeStruct((t, D_MODEL), F32)] * 2,
        compiler_params=_cparams(1),
        name=name,
    )(*args)


SUB = 8

_CELLS = [(i, j) for i in range(TOPK) for j in range(TOPK) if (i + 1) * (j + 1) <= TOPK]
_CELLS = _CELLS + [None] * (SUB * pl.next_power_of_2(pl.cdiv(len(_CELLS), SUB)) - len(_CELLS))
_CELL_GROUPS = [_CELLS[k:k + SUB] for k in range(0, len(_CELLS), SUB)]


def _stack_rows(rows, r_iota):
    out = jnp.broadcast_to(rows[-1], r_iota.shape)
    for r in range(SUB - 2, -1, -1):
        if rows[r] is not rows[r + 1]:
            out = jnp.where(r_iota <= r, rows[r], out)
    return out


def _row_iota(shape):
    return lax.broadcasted_iota(jnp.int32, shape, 0).astype(F32)


def _sort_network(lo, hi):
    def merge(lo, hi, r):
        step = 2 * r
        if step < hi - lo:
            yield from merge(lo, hi, step)
            yield from merge(lo + r, hi, step)
            yield from ((i, i + r) for i in range(lo + r, hi - r, step))
        else:
            yield (lo, lo + r)

    if hi > lo:
        mid = lo + (hi - lo) // 2
        yield from _sort_network(lo, mid)
        yield from _sort_network(mid + 1, hi)
        yield from merge(lo, hi, 1)


def _top_rows(s, k, tag=None):
    n = s.shape[0]
    depth = n // SUB
    iota = _row_iota(s.shape)
    split = lambda a: [a[v * SUB:(v + 1) * SUB, :] for v in range(depth)]
    cols = [split(-s), split(iota)] + ([] if tag is None else [split(tag)])
    neg, idx = cols[0], cols[1]
    for a, b in _sort_network(0, depth - 1):
        tie = neg[a] == neg[b]
        swap = jnp.where(tie, idx[b], neg[b]) < jnp.where(tie, idx[a], neg[a])
        for c in cols:
            c[a], c[b] = jnp.where(swap, c[b], c[a]), jnp.where(swap, c[a], c[b])
    vals, tags = [], []
    for r in range(k):
        m = jnp.min(neg[0], axis=0, keepdims=True)
        am = jnp.min(jnp.where(neg[0] == m, idx[0], float(n)), axis=0, keepdims=True)
        won = idx[0] == am
        vals.append(-m)
        tags.append(am if tag is None else jnp.max(jnp.where(won, cols[2][0], -1.0), axis=0, keepdims=True))
        for d in range(min(depth - 1, k - 1 - r)):
            for c in cols:
                c[d] = jnp.where(won, c[d + 1], c[d])
    return vals, tags


def _retrieve_kernel(hf_ref, wq_ref, keys_ref, idx_ref, g_ref, g_t_ref):
    q = jnp.dot(hf_ref[...].astype(BF16), wq_ref[...], preferred_element_type=F32).astype(BF16)
    r_iota = lax.broadcasted_iota(jnp.int32, (SUB, TT), 0)
    neg_inf = jnp.full((1, TT), -jnp.inf, F32)
    zero = jnp.zeros((1, TT), F32)
    for h in range(N_PEER_HEADS):
        halves = []
        for p in range(2):
            hp = h * 2 + p
            s = lax.dot_general(keys_ref[hp], q[:, hp * HALF_KEY:(hp + 1) * HALF_KEY], _NT,
                                preferred_element_type=F32)
            halves.append(_top_rows(s, TOPK))
        (v1, i1), (v2, i2) = halves
        i1 = [iv * float(N_KEYS) for iv in i1]
        cand, cidx = [], []
        for cells in _CELL_GROUPS:
            pick = lambda rows, pos, pad: _stack_rows([pad if c is None else rows[c[pos]] for c in cells], r_iota)
            cand.append(pick(v1, 0, neg_inf) + pick(v2, 1, neg_inf))
            cidx.append(pick(i1, 0, zero) + pick(i2, 1, zero))
        cand = jnp.concatenate(cand, axis=0)
        cidx = jnp.concatenate(cidx, axis=0)
        top_s, experts = _top_rows(cand, TOPK, tag=cidx)
        e =[jnp.exp(_stack_rows(top_s[k:k + SUB], r_iota) - top_s[0]) for k in range(0, TOPK, SUB)]
        denom = jnp.sum(sum(e[1:], e[0]), axis=0, keepdims=True)
        for k, ev in enumerate(e):
            rows = slice(h * TOPK + k * SUB, h * TOPK + (k + 1) * SUB)
            g_t_ref[rows, :] = ev / denom
            ids = _stack_rows(experts[k * SUB:(k + 1) * SUB], r_iota)
            idx_ref[rows, :] = ids.astype(jnp.int32) * SLABS
    g_ref[...] = g_t_ref[...].T


def _retrieve_call(hf, wq_bf16, keys_bf16):
    t = hf.shape[0]
    nq = wq_bf16.shape[1]
    return pl.pallas_call(
        _retrieve_kernel,
        grid=(t // TT,),
        in_specs=[
            pl.BlockSpec((TT, D_MODEL), lambda i: (i, 0)),
            pl.BlockSpec((D_MODEL, nq), lambda i: (0, 0)),
            pl.BlockSpec((2 * N_PEER_HEADS, N_KEYS, HALF_KEY), lambda i: (0, 0, 0)),
        ],
        out_specs=[
            pl.BlockSpec((N_SEL, TT), lambda i: (0, i)),
            pl.BlockSpec((TT, N_SEL), lambda i: (i, 0)),
        ],
        out_shape=[
            jax.ShapeDtypeStruct((N_SEL, t), jnp.int32),
            jax.ShapeDtypeStruct((t, N_SEL), F32),
        ],
        scratch_shapes=[pltpu.VMEM((N_SEL, TT), F32)],
        compiler_params=_cparams(1),
        name="peer_retrieve",
    )(hf, wq_bf16, keys_bf16)


def _pack_kernel(t_ref, o_ref):
    x = t_ref[0]
    w = pltpu.pack_elementwise([x[:, :WORDS], x[:, WORDS:]], packed_dtype=BF16)
    for j in range(SLABS):
        o_ref[pl.ds(j, TM, stride=SLABS), :] = w[:, j * LANES:(j + 1) * LANES]


def _pack_table(tabs, layer):
    return pl.pallas_call(
        _pack_kernel,
        grid=(N_EXPERTS // TM,),
        in_specs=[pl.BlockSpec((1, TM, D_MODEL), lambda i: (layer, i, 0))],
        out_specs=pl.BlockSpec((TM * SLABS, LANES), lambda i: (i, 0)),
        out_shape=jax.ShapeDtypeStruct((N_EXPERTS * SLABS, LANES), jnp.uint32),
        compiler_params=_cparams(1),
        name="peer_pack_table",
    )(tabs)


def _gather_token(idx_ref, t, tab_ref, stage_ref):
    for k in range(N_SEL):
        i = pl.multiple_of(idx_ref.at[k][t], SLABS)
        stage_ref[pl.ds(k * SLABS, SLABS), :] = tab_ref[pl.ds(i, SLABS), :]


def _unpack_words(w, half):
    return pltpu.unpack_elementwise(w, index=half, packed_dtype=BF16, unpacked_dtype=F32)


def _unpack_stage(stage_ref):
    his, los = [], []
    for j in range(SLABS):
        w = stage_ref[pl.ds(j, N_SEL, stride=SLABS), :]
        his.append(_unpack_words(w, 0).astype(BF16))
        los.append(_unpack_words(w, 1).astype(BF16))
    return jnp.concatenate(his + los, axis=1)


def _token_pipeline(idx_hbm, bufs, sems, tab_ref, stages, produce, consume):
    step, n_steps = pl.program_id(0), pl.num_programs(0)
    n_groups = TT // N_STAGE

    def copy(tile, slot):
        cols = pl.ds(pl.multiple_of(tile * TT, TT), TT)
        return pltpu.make_async_copy(idx_hbm.at[:, cols], bufs[slot], sems.at[slot])

    def gather_group(idx_ref, t0):
        for q in range(N_STAGE):
            _gather_token(idx_ref, t0 + q, tab_ref, stages[q])

    def group(t0, refill):
        results = [produce(t0 + q, stages[q]) for q in range(N_STAGE)]
        refill()
        consume(t0, results)

    @pl.when(step == 0)
    def _():
        copy(0, 0).start()
        copy(0, 0).wait()
        gather_group(bufs[0], 0)

    for slot in range(2):
        @pl.when(lax.rem(step, 2) == slot)
        def _(slot=slot):
            has_next = step + 1 < n_steps

            @pl.when(has_next)
            def _():
                copy(step + 1, 1 - slot).start()

            def body(i, carry):
                t0 = i * N_STAGE
                group(t0, lambda: gather_group(bufs[slot], t0 + N_STAGE))
                return carry

            lax.fori_loop(0, n_groups - 1, body, 0)

            @pl.when(has_next)
            def _():
                copy(step + 1, 1 - slot).wait()
                group(TT - N_STAGE, lambda: gather_group(bufs[1 - slot], 0))

            @pl.when(jnp.logical_not(has_next))
            def _():
                group(TT - N_STAGE, lambda: None)


def _peer_u_kernel(idx_hbm, x_ref, tab_ref, act_ref, xb_ref, act_t_ref, idx0, idx1, sems, *stages):
    xb_ref[...] = x_ref[...].astype(BF16).astype(F32)
    act_t_ref[...] = jnp.zeros((N_SEL, TT), F32)
    lane = lax.broadcasted_iota(jnp.int32, (N_SEL, TT), 1)

    def produce(t, stage):
        xr = xb_ref[pl.ds(t, 1), :]
        p = None
        for j in range(SLABS):
            w = stage[pl.ds(j, N_SEL, stride=SLABS), :]
            term = (_unpack_words(w, 0) * xr[:, j * LANES:(j + 1) * LANES]
                    + _unpack_words(w, 1) * xr[:, WORDS + j * LANES:WORDS + (j + 1) * LANES])
            p = term if p is None else p + term
        return jnp.sum(p, axis=1, keepdims=True)

    def consume(t0, cols):
        acc = act_t_ref[...]
        for q, r in enumerate(cols):
            acc = jnp.where(lane == t0 + q, r, acc)
        act_t_ref[...] = acc

    _token_pipeline(idx_hbm, (idx0, idx1), sems, tab_ref, stages, produce, consume)
    act_ref[...] = act_t_ref[...].T


_INDEX_SCRATCH = [pltpu.SMEM((N_SEL, TT), jnp.int32), pltpu.SMEM((N_SEL, TT), jnp.int32),
                  pltpu.SemaphoreType.DMA((2,))]
_STAGE_SCRATCH = [pltpu.VMEM((SLABS * N_SEL, LANES), jnp.uint32)] * N_STAGE


def _peer_u_call(idx, hf, tab):
    t = hf.shape[0]
    return pl.pallas_call(
        _peer_u_kernel,
        grid=(t // TT,),
        in_specs=[
            pl.BlockSpec(memory_space=pl.ANY),
            pl.BlockSpec((TT, D_MODEL), lambda i: (i, 0)),
            pl.BlockSpec(memory_space=pltpu.VMEM),
        ],
        out_specs=pl.BlockSpec((TT, N_SEL), lambda i: (i, 0)),
        out_shape=jax.ShapeDtypeStruct((t, N_SEL), F32),
        scratch_shapes=[pltpu.VMEM((TT, D_MODEL), F32), pltpu.VMEM((N_SEL, TT), F32)]
        + _INDEX_SCRATCH + _STAGE_SCRATCH,
        compiler_params=_cparams(1),
        name="peer_u",
    )(idx, hf, tab)


def _peer_v_kernel(idx_hbm, act_ref, g_ref, tab_ref, x_ref, gf_ref, lng_ref, lnb_ref, o_ref, w_ref, f_ref,
                   idx0, idx1, sems, *stages):
    w_ref[...] = g_ref[...] * _gelu(act_ref[...])

    def produce(t, stage):
        wr = jnp.broadcast_to(w_ref[pl.ds(t, 1), :], (8, N_SEL)).astype(BF16)
        return jnp.dot(wr, _unpack_stage(stage), preferred_element_type=F32)

    def consume(t0, rows):
        for q, f in enumerate(rows):
            f_ref[pl.ds(t0 + q, 1), :] = f[0:1]

    _token_pipeline(idx_hbm, (idx0, idx1), sems, tab_ref, stages, produce, consume)
    z = DEEPNORM_ALPHA * x_ref[...] + gf_ref[0] * f_ref[...]
    o_ref[...] = _layer_norm(z, lng_ref[...], lnb_ref[...])


def _peer_v_call(idx, act, g, tab, x, gf, lng, lnb, n_prompt_tiles):
    t = x.shape[0]
    sel = lambda i: (jnp.where(i >= n_prompt_tiles, 1, 0), 0, 0)
    row = pl.BlockSpec((1, D_MODEL), lambda i: (0, 0))
    return pl.pallas_call(
        _peer_v_kernel,
        grid=(t // TT,),
        in_specs=[
            pl.BlockSpec(memory_space=pl.ANY),
            pl.BlockSpec((TT, N_SEL), lambda i: (i, 0)),
            pl.BlockSpec((TT, N_SEL), lambda i: (i, 0)),
            pl.BlockSpec(memory_space=pltpu.VMEM),
            pl.BlockSpec((TT, D_MODEL), lambda i: (i, 0)),
            pl.BlockSpec((1, GROUP, D_MODEL), sel),
            row, row,
        ],
        out_specs=pl.BlockSpec((TT, D_MODEL), lambda i: (i, 0)),
        out_shape=jax.ShapeDtypeStruct((t, D_MODEL), F32),
        scratch_shapes=[pltpu.VMEM((TT, N_SEL), F32), pltpu.VMEM((TT, D_MODEL), F32)]
        + _INDEX_SCRATCH + _STAGE_SCRATCH,
        compiler_params=_cparams(1),
        name="peer_v",
    )(idx, act, g, tab, x, gf, lng.reshape(1, D_MODEL), lnb.reshape(1, D_MODEL))


def _time_major(a):
    return jnp.swapaxes(a, 0, 1).reshape(a.shape[0] * a.shape[1], a.shape[2])


def _batch_major(a, nb):
    return jnp.swapaxes(a.reshape(a.shape[0] // nb, nb, a.shape[1]), 0, 1)


def kernel(x_prompt, x_sample, state_lru_h, state_lru_conv, state_conf_conv, c_prompt, c_sample, w_ada, b_ada, ln_mix_g, ln_mix_b, ln_ffn_g, ln_ffn_b, lru_w_in, lru_conv_w, lru_conv_b, lru_w_a, lru_b_a, lru_w_i, lru_b_i, lru_lambda, lru_w_out, conf_w_pw1, conf_b_pw1, conf_dw_w, conf_dw_b, conf_ln_g, conf_ln_b, conf_w_pw2, conf_b_pw2, peer_w_q, peer_keys, peer_u, peer_v):
    bp, lp, _ = x_prompt.shape
    bs, ls, _ = x_sample.shape
    tp, ts = bp * lp, bs * ls
    assert GROUP % bp == 0 and bs == GROUP and TM % GROUP == 0
    for nb, seq in ((bp, lp), (bs, ls)):
        assert (seq * nb) % TM == 0 and (seq * nb == TM or TM // nb >= CONF_CONV_W - 1)
    npt_m, npt_t = tp // TM, tp // TT

    x = jnp.concatenate([_time_major(x_prompt), _time_major(x_sample)], axis=0)

    mod = _mod_call(jnp.concatenate([c_prompt, c_sample], axis=0), w_ada, b_ada)
    mod = mod.reshape(DEPTH, bp + bs, 6, D_MODEL)
    pat = jnp.stack([jnp.tile(mod[:, :bp], (1, GROUP // bp, 1, 1)), mod[:, bp:]], axis=1)
    pat = jnp.transpose(pat, (0, 3, 1, 2, 4))
    mods = [[pat[i, k] for k in range(6)] for i in range(DEPTH)]

    zeros = lambda *s: jnp.zeros(s, F32)
    new_h, new_lconv, new_cconv = [], [], []
    for i in range(DEPTH):
        sh_m, sc_m, g_m, sh_f, sc_f, g_f = mods[i]
        j = i // 2
        if i % 2 == 0:
            proj = _mm_mod_call(x, sc_m, sh_m, lru_w_in[j].astype(BF16), None, npt_m, False, "lru_in_proj")
            wa, wi = lru_w_a[j].astype(BF16), lru_w_i[j].astype(BF16)
            common = (lru_conv_w[j], lru_conv_b[j], wa, lru_b_a[j], wi, lru_b_i[j], lru_lambda[j])
            yp, hp, cp = _lru_call(proj, zeros((LRU_CONV_W - 1) * bp, D_RNN), zeros(bp, D_RNN), *common,
                                   nb=bp, seq=lp, row_block0=0, name="lru_seq_prompt")
            ys, hs, cs = _lru_call(proj, _time_major(state_lru_conv[j]), state_lru_h[j], *common,
                                   nb=bs, seq=ls, row_block0=npt_m, name="lru_seq_sample")
            new_h.append((hp, hs))
            new_lconv.append((_batch_major(cp, bp), _batch_major(cs, bs)))
            mixed, w_out, b_out = (yp, ys), lru_w_out[j], None
        else:
            u = _mm_mod_call(x, sc_m, sh_m, conf_w_pw1[j].astype(BF16), conf_b_pw1[j], npt_m, True, "conf_pw1_glu")
            common = (conf_dw_w[j], conf_dw_b[j], conf_ln_g[j], conf_ln_b[j])
            dp, cp = _conf_call(u, zeros((CONF_CONV_W - 1) * bp, D_CONF), *common,
                                nb=bp, seq=lp, row_block0=0, name="conf_seq_prompt")
            ds, cs = _conf_call(u, _time_major(state_conf_conv[j]), *common,
                                nb=bs, seq=ls, row_block0=npt_m, name="conf_seq_sample")
            new_cconv.append((_batch_major(cp, bp), _batch_major(cs, bs)))
            mixed, w_out, b_out = (dp, ds), conf_w_pw2[j], conf_b_pw2[j]

        x, hf = _proj_ln_call(*mixed, w_out.astype(BF16), b_out, x, g_m, ln_mix_g[i], ln_mix_b[i], sc_f, sh_f,
                              npt_m, "mix_out_ln")
        idx, gate = _retrieve_call(hf, peer_w_q[i].astype(BF16),
                                   peer_keys[i].astype(BF16).reshape(2 * N_PEER_HEADS, N_KEYS, HALF_KEY))
        act = _peer_u_call(idx, hf, _pack_table(peer_u, i))
        x = _peer_v_call(idx, act, gate, _pack_table(peer_v, i), x, g_f, ln_ffn_g[i], ln_ffn_b[i], npt_t)

    y_prompt = _batch_major(x[:tp], bp)
    y_sample = _batch_major(x[tp:], bs)
    pick = lambda pairs, k: jnp.stack([p[k] for p in pairs])
    return (y_prompt, y_sample, pick(new_h, 0), pick(new_lconv, 0), pick(new_cconv, 0),
            pick(new_h, 1), pick(new_lconv, 1), pick(new_cconv, 1))
```

```python
import functools
import math

import jax
import jax.numpy as jnp
from jax import lax
from jax.experimental import pallas as pl
from jax.experimental.pallas import tpu as pltpu

F32 = jnp.float32
BF16 = jnp.bfloat16

D_MODEL = 1024
DEPTH = 2
D_RNN = 1280
N_RNN_HEADS = 10
RNN_HEAD_DIM = 128
LRU_CONV_W = 4
LRU_C = 8.0
D_CONF = 1024
CONF_CONV_W = 31
N_PEER_HEADS = 8
N_KEYS = 128
N_EXPERTS = N_KEYS * N_KEYS
HALF_KEY = 128
TOPK = 16
N_SEL = N_PEER_HEADS * TOPK
DEEPNORM_ALPHA = (2 * DEPTH) ** 0.25
LN_EPS = 1e-5

LANES = 128
GROUP = 128
TM = 512
TT = 128
WORDS = D_MODEL // 2
SLABS = WORDS // LANES
N_STAGE = 4
VMEM_LIMIT = 56 * 1024 * 1024

_NT = (((1,), (1,)), ((), ()))


def _cparams(n_axes, vmem=VMEM_LIMIT):
    return pltpu.CompilerParams(dimension_semantics=("arbitrary",) * n_axes, vmem_limit_bytes=vmem)


def _gelu(x):
    return 0.5 * x * (1.0 + lax.erf(x * (1.0 / math.sqrt(2.0))))


def _affine(x, scale, shift=None):
    rows, c = x.shape
    xr = x.reshape(rows // GROUP, GROUP, c) * scale[None]
    if shift is not None:
        xr = xr + shift[None]
    return xr.reshape(rows, c)


def _layer_norm(z, g, b):
    mu = jnp.mean(z, axis=-1, keepdims=True)
    zc = z - mu
    var = jnp.mean(zc * zc, axis=-1, keepdims=True)
    return zc * lax.rsqrt(var + LN_EPS) * g + b


def _mod_kernel(c_ref, w_ref, b_ref, o_ref):
    c = c_ref[...]
    a = (c * jax.nn.sigmoid(c)).astype(BF16)
    o_ref[0] = jnp.dot(a, w_ref[0].astype(BF16), preferred_element_type=F32) + b_ref[0]


def _mod_call(c_all, w_ada, b_ada):
    nb = c_all.shape[0]
    tn = 1536
    return pl.pallas_call(
        _mod_kernel,
        grid=(DEPTH, 6 * D_MODEL // tn),
        in_specs=[
            pl.BlockSpec((nb, D_MODEL), lambda l, j: (0, 0)),
            pl.BlockSpec((1, D_MODEL, tn), lambda l, j: (l, 0, j)),
            pl.BlockSpec((1, 1, tn), lambda l, j: (l, 0, j)),
        ],
        out_specs=pl.BlockSpec((1, nb, tn), lambda l, j: (l, 0, j)),
        out_shape=jax.ShapeDtypeStruct((DEPTH, nb, 6 * D_MODEL), F32),
        compiler_params=_cparams(2),
        name="adaln_mod",
    )(c_all, w_ada, b_ada.reshape(DEPTH, 1, 6 * D_MODEL))


def _mm_mod_kernel(x_ref, sc_ref, sh_ref, w_ref, *rest, glu):
    o_ref = rest[-1]
    xm = _affine(x_ref[...], 1.0 + sc_ref[0], sh_ref[0])
    p = jnp.dot(xm.astype(BF16), w_ref[...], preferred_element_type=F32)
    if len(rest) == 2:
        p = p + rest[0][...]
    if glu:
        n = p.shape[-1] // 2
        p = p[:, :n] * jax.nn.sigmoid(p[:, n:])
    o_ref[...] = p


def _mm_mod_call(x, sc, sh, w_bf16, bias, n_prompt_tiles, glu, name):
    t = x.shape[0]
    n = w_bf16.shape[1]
    n_out = n // 2 if glu else n
    sel = lambda i: (jnp.where(i >= n_prompt_tiles, 1, 0), 0, 0)
    in_specs = [
        pl.BlockSpec((TM, D_MODEL), lambda i: (i, 0)),
        pl.BlockSpec((1, GROUP, D_MODEL), sel),
        pl.BlockSpec((1, GROUP, D_MODEL), sel),
        pl.BlockSpec((D_MODEL, n), lambda i: (0, 0)),
    ]
    args = [x, sc, sh, w_bf16]
    if bias is not None:
        in_specs.append(pl.BlockSpec((1, n), lambda i: (0, 0)))
        args.append(bias.reshape(1, n))
    return pl.pallas_call(
        functools.partial(_mm_mod_kernel, glu=glu),
        grid=(t // TM,),
        in_specs=in_specs,
        out_specs=pl.BlockSpec((TM, n_out), lambda i: (i, 0)),
        out_shape=jax.ShapeDtypeStruct((t, n_out), F32),
        compiler_params=_cparams(1),
        name=name,
    )(*args)


def _lru_kernel(proj_ref, conv0_ref, h0_ref, cw_ref, cb_ref, wa_ref, ba_ref, wi_ref, bi_ref, lam_ref,
                yg_ref, hnew_ref, convnew_ref, win_ref, xc_ref, a_ref, b_ref, h_ref, *, nb, tl, cw, n_tiles):
    rows = tl * nb
    halo = (LRU_CONV_W - 1) * nb

    @pl.when(pl.program_id(0) == 0)
    def _():
        win_ref[0:halo, :] = conv0_ref[...]
        h_ref[...] = h0_ref[...]

    win_ref[halo:halo + rows, :] = proj_ref[:, :D_RNN]
    xc = cb_ref[...] + win_ref[0:rows, :] * cw_ref[0:1, :]
    for w in range(1, LRU_CONV_W):
        xc = xc + win_ref[w * nb:w * nb + rows, :] * cw_ref[w:w + 1, :]
    xc_ref[...] = xc
    carry = win_ref[rows:rows + halo, :]
    convnew_ref[...] = carry
    if n_tiles > 1:
        win_ref[0:halo, :] = carry

    lam = lam_ref[...]
    neg = -lam
    softplus = jnp.maximum(neg, 0.0) + jnp.log1p(jnp.exp(-jnp.abs(neg)))
    for hd in range(N_RNN_HEADS):
        cs = slice(hd * RNN_HEAD_DIM, (hd + 1) * RNN_HEAD_DIM)
        xh = xc_ref[:, cs]
        xb = xh.astype(BF16)
        r = jax.nn.sigmoid(jnp.dot(xb, wa_ref[hd], preferred_element_type=F32) + ba_ref[:, cs])
        ig = jax.nn.sigmoid(jnp.dot(xb, wi_ref[hd], preferred_element_type=F32) + bi_ref[:, cs])
        log_a = (-LRU_C * r) * softplus[:, cs]
        th = jnp.tanh(log_a)
        a_ref[:, cs] = jnp.exp(log_a)
        b_ref[:, cs] = jnp.sqrt(-2.0 * th / (1.0 - th)) * (ig * xh)

    for c0 in range(0, D_RNN, cw):
        cs = slice(c0, c0 + cw)

        def step(l, h, cs=cs):
            rs = pl.ds(pl.multiple_of(l * nb, nb), nb)
            h = a_ref[rs, cs] * h + b_ref[rs, cs]
            b_ref[rs, cs] = h
            return h

        h_ref[:, cs] = lax.fori_loop(0, tl, step, h_ref[:, cs])

    yg_ref[...] = b_ref[...] * _gelu(proj_ref[:, D_RNN:])
    hnew_ref[...] = h_ref[...]


def _lru_call(proj, conv0, h0, cw, cb, wa, ba, wi, bi, lam, *, nb, seq, row_block0, name):
    tl = TM // nb
    n_tiles = seq // tl
    halo = (LRU_CONV_W - 1) * nb
    chan = D_RNN if nb * D_RNN <= 16 * 1024 else LANES
    full = lambda shape: pl.BlockSpec(shape, lambda i: (0,) * len(shape))
    return pl.pallas_call(
        functools.partial(_lru_kernel, nb=nb, tl=tl, cw=chan, n_tiles=n_tiles),
        grid=(n_tiles,),
        in_specs=[
            pl.BlockSpec((TM, 2 * D_RNN), lambda i: (row_block0 + i, 0)),
            full((halo, D_RNN)),
            full((nb, D_RNN)),
            full((LRU_CONV_W, D_RNN)),
            full((1, D_RNN)),
            full((N_RNN_HEADS, RNN_HEAD_DIM, RNN_HEAD_DIM)),
            full((1, D_RNN)),
            full((N_RNN_HEADS, RNN_HEAD_DIM, RNN_HEAD_DIM)),
            full((1, D_RNN)),
            full((1, D_RNN)),
        ],
        out_specs=[
            pl.BlockSpec((TM, D_RNN), lambda i: (i, 0)),
            full((nb, D_RNN)),
            full((halo, D_RNN)),
        ],
        out_shape=[
            jax.ShapeDtypeStruct((seq * nb, D_RNN), F32),
            jax.ShapeDtypeStruct((nb, D_RNN), F32),
            jax.ShapeDtypeStruct((halo, D_RNN), F32),
        ],
        scratch_shapes=[
            pltpu.VMEM((halo + TM, D_RNN), F32),
            pltpu.VMEM((TM, D_RNN), F32),
            pltpu.VMEM((TM, D_RNN), F32),
            pltpu.VMEM((TM, D_RNN), F32),
            pltpu.VMEM((nb, D_RNN), F32),
        ],
        compiler_params=_cparams(1),
        name=name,
    )(proj, conv0, h0, cw, cb.reshape(1, D_RNN), wa, ba.reshape(1, D_RNN), wi, bi.reshape(1, D_RNN),
      lam.reshape(1, D_RNN))


def _conf_kernel(u_ref, conv0_ref, dw_ref, db_ref, g_ref, b_ref, o_ref, convnew_ref, win_ref, d_ref,
                 *, nb, tl, n_tiles):
    rows = tl * nb
    halo = (CONF_CONV_W - 1) * nb
    rc = 64
    cc = 256

    @pl.when(pl.program_id(0) == 0)
    def _():
        win_ref[0:halo, :] = conv0_ref[...]

    win_ref[halo:halo + rows, :] = u_ref[...]

    for c0 in range(0, D_CONF, cc):
        cs = slice(c0, c0 + cc)

        def chunk(r, carry, cs=cs):
            r0 = pl.multiple_of(r * rc, rc)
            acc = jnp.broadcast_to(db_ref[:, cs], (rc, cc))
            for w in range(CONF_CONV_W):
                acc = acc + win_ref[pl.ds(r0 + w * nb, rc), cs] * dw_ref[w:w + 1, cs]
            d_ref[pl.ds(r0, rc), cs] = acc
            return carry

        lax.fori_loop(0, rows // rc, chunk, 0)

    convnew_ref[...] = win_ref[rows:rows + halo, :]
    if n_tiles > 1:
        win_ref[0:halo, :] = win_ref[rows:rows + halo, :]

    y = _layer_norm(d_ref[...], g_ref[...], b_ref[...])
    o_ref[...] = y * jax.nn.sigmoid(y)


def _conf_call(u, conv0, dw, db, g, b, *, nb, seq, row_block0, name):
    tl = TM // nb
    n_tiles = seq // tl
    halo = (CONF_CONV_W - 1) * nb
    full = lambda shape: pl.BlockSpec(shape, lambda i: (0,) * len(shape))
    return pl.pallas_call(
        functools.partial(_conf_kernel, nb=nb, tl=tl, n_tiles=n_tiles),
        grid=(n_tiles,),
        in_specs=[
            pl.BlockSpec((TM, D_CONF), lambda i: (row_block0 + i, 0)),
            full((halo, D_CONF)),
            full((CONF_CONV_W, D_CONF)),
            full((1, D_CONF)),
            full((1, D_CONF)),
            full((1, D_CONF)),
        ],
        out_specs=[
            pl.BlockSpec((TM, D_CONF), lambda i: (i, 0)),
            full((halo, D_CONF)),
        ],
        out_shape=[
            jax.ShapeDtypeStruct((seq * nb, D_CONF), F32),
            jax.ShapeDtypeStruct((halo, D_CONF), F32),
        ],
        scratch_shapes=[
            pltpu.VMEM((halo + TM, D_CONF), F32),
            pltpu.VMEM((TM, D_CONF), F32),
        ],
        compiler_params=_cparams(1),
        name=name,
    )(u, conv0, dw, db.reshape(1, D_CONF), g.reshape(1, D_CONF), b.reshape(1, D_CONF))


def _proj_ln_kernel(yp_ref, ys_ref, w_ref, x_ref, gm_ref, lng_ref, lnb_ref, scf_ref, shf_ref, *rest, n_prompt_tiles):
    x1_ref, hf_ref = rest[-2:]
    y = jnp.where(pl.program_id(0) < n_prompt_tiles, yp_ref[...], ys_ref[...])
    out = jnp.dot(y.astype(BF16), w_ref[...], preferred_element_type=F32)
    if len(rest) == 3:
        out = out + rest[0][...]
    z = DEEPNORM_ALPHA * x_ref[...] + _affine(out, gm_ref[0])
    x1 = _layer_norm(z, lng_ref[...], lnb_ref[...])
    x1_ref[...] = x1
    hf_ref[...] = _affine(x1, 1.0 + scf_ref[0], shf_ref[0])


def _proj_ln_call(y_prompt, y_sample, w_bf16, bias, x, gm, lng, lnb, scf, shf, n_prompt_tiles, name):
    t, k = x.shape[0], y_prompt.shape[1]
    sel = lambda i: (jnp.where(i >= n_prompt_tiles, 1, 0), 0, 0)
    row = pl.BlockSpec((1, D_MODEL), lambda i: (0, 0))
    mod = pl.BlockSpec((1, GROUP, D_MODEL), sel)
    tile = pl.BlockSpec((TM, D_MODEL), lambda i: (i, 0))
    in_specs = [
        pl.BlockSpec((TM, k), lambda i: (jnp.minimum(i, n_prompt_tiles - 1), 0)),
        pl.BlockSpec((TM, k), lambda i: (jnp.maximum(i - n_prompt_tiles, 0), 0)),
        pl.BlockSpec((k, D_MODEL), lambda i: (0, 0)),
        tile, mod, row, row, mod, mod,
    ]
    args = [y_prompt, y_sample, w_bf16, x, gm, lng.reshape(1, D_MODEL), lnb.reshape(1, D_MODEL), scf, shf]
    if bias is not None:
        in_specs.append(row)
        args.append(bias.reshape(1, D_MODEL))
    return pl.pallas_call(
        functools.partial(_proj_ln_kernel, n_prompt_tiles=n_prompt_tiles),
        grid=(t // TM,),
        in_specs=in_specs,
        out_specs=[tile, tile],
        out_shape=[jax.ShapeDtypeStruct((t, D_MODEL), F32)] * 2,
        compiler_params=_cparams(1),
        name=name,
    )(*args)


SUB = 8

_CELLS = [(i, j) for i in range(TOPK) for j in range(TOPK) if (i + 1) * (j + 1) <= TOPK]
_CELLS = _CELLS + [None] * (SUB * pl.next_power_of_2(pl.cdiv(len(_CELLS), SUB)) - len(_CELLS))
_CELL_GROUPS = [_CELLS[k:k + SUB] for k in range(0, len(_CELLS), SUB)]


def _stack_rows(rows, r_iota):
    out = jnp.broadcast_to(rows[-1], r_iota.shape)
    for r in range(SUB - 2, -1, -1):
        if rows[r] is not rows[r + 1]:
            out = jnp.where(r_iota <= r, rows[r], out)
    return out


def _row_iota(shape):
    return lax.broadcasted_iota(jnp.int32, shape, 0).astype(F32)


def _sort_network(lo, hi):
    def merge(lo, hi, r):
        step = 2 * r
        if step < hi - lo:
            yield from merge(lo, hi, step)
            yield from merge(lo + r, hi, step)
            yield from ((i, i + r) for i in range(lo + r, hi - r, step))
        else:
            yield (lo, lo + r)

    if hi > lo:
        mid = lo + (hi - lo) // 2
        yield from _sort_network(lo, mid)
        yield from _sort_network(mid + 1, hi)
        yield from merge(lo, hi, 1)


def _top_rows(s, k, tag=None):
    n = s.shape[0]
    depth = n // SUB
    iota = _row_iota(s.shape)
    split = lambda a: [a[v * SUB:(v + 1) * SUB, :] for v in range(depth)]
    cols = [split(-s), split(iota)] + ([] if tag is None else [split(tag)])
    neg, idx = cols[0], cols[1]
    for a, b in _sort_network(0, depth - 1):
        tie = neg[a] == neg[b]
        swap = jnp.where(tie, idx[b], neg[b]) < jnp.where(tie, idx[a], neg[a])
        for c in cols:
            c[a], c[b] = jnp.where(swap, c[b], c[a]), jnp.where(swap, c[a], c[b])
    vals, tags = [], []
    for r in range(k):
        m = jnp.min(neg[0], axis=0, keepdims=True)
        am = jnp.min(jnp.where(neg[0] == m, idx[0], float(n)), axis=0, keepdims=True)
        won = idx[0] == am
        vals.append(-m)
        tags.append(am if tag is None else jnp.max(jnp.where(won, cols[2][0], -1.0), axis=0, keepdims=True))
        for d in range(min(depth - 1, k - 1 - r)):
            for c in cols:
                c[d] = jnp.where(won, c[d + 1], c[d])
    return vals, tags


def _retrieve_kernel(hf_ref, wq_ref, keys_ref, idx_ref, g_ref, g_t_ref):
    q = jnp.dot(hf_ref[...].astype(BF16), wq_ref[...], preferred_element_type=F32).astype(BF16)
    r_iota = lax.broadcasted_iota(jnp.int32, (SUB, TT), 0)
    neg_inf = jnp.full((1, TT), -jnp.inf, F32)
    zero = jnp.zeros((1, TT), F32)
    for h in range(N_PEER_HEADS):
        halves = []
        for p in range(2):
            hp = h * 2 + p
            s = lax.dot_general(keys_ref[hp], q[:, hp * HALF_KEY:(hp + 1) * HALF_KEY], _NT,
                                preferred_element_type=F32)
            halves.append(_top_rows(s, TOPK))
        (v1, i1), (v2, i2) = halves
        i1 = [iv * float(N_KEYS) for iv in i1]
        cand, cidx = [], []
        for cells in _CELL_GROUPS:
            pick = lambda rows, pos, pad: _stack_rows([pad if c is None else rows[c[pos]] for c in cells], r_iota)
            cand.append(pick(v1, 0, neg_inf) + pick(v2, 1, neg_inf))
            cidx.append(pick(i1, 0, zero) + pick(i2, 1, zero))
        cand = jnp.concatenate(cand, axis=0)
        cidx = jnp.concatenate(cidx, axis=0)
        top_s, experts = _top_rows(cand, TOPK, tag=cidx)
        e =[jnp.exp(_stack_rows(top_s[k:k + SUB], r_iota) - top_s[0]) for k in range(0, TOPK, SUB)]
        denom = jnp.sum(sum(e[1:], e[0]), axis=0, keepdims=True)
        for k, ev in enumerate(e):
            rows = slice(h * TOPK + k * SUB, h * TOPK + (k + 1) * SUB)
            g_t_ref[rows, :] = ev / denom
            ids = _stack_rows(experts[k * SUB:(k + 1) * SUB], r_iota)
            idx_ref[rows, :] = ids.astype(jnp.int32) * SLABS
    g_ref[...] = g_t_ref[...].T


def _retrieve_call(hf, wq_bf16, keys_bf16):
    t = hf.shape[0]
    nq = wq_bf16.shape[1]
    return pl.pallas_call(
        _retrieve_kernel,
        grid=(t // TT,),
        in_specs=[
            pl.BlockSpec((TT, D_MODEL), lambda i: (i, 0)),
            pl.BlockSpec((D_MODEL, nq), lambda i: (0, 0)),
            pl.BlockSpec((2 * N_PEER_HEADS, N_KEYS, HALF_KEY), lambda i: (0, 0, 0)),
        ],
        out_specs=[
            pl.BlockSpec((N_SEL, TT), lambda i: (0, i)),
            pl.BlockSpec((TT, N_SEL), lambda i: (i, 0)),
        ],
        out_shape=[
            jax.ShapeDtypeStruct((N_SEL, t), jnp.int32),
            jax.ShapeDtypeStruct((t, N_SEL), F32),
        ],
        scratch_shapes=[pltpu.VMEM((N_SEL, TT), F32)],
        compiler_params=_cparams(1),
        name="peer_retrieve",
    )(hf, wq_bf16, keys_bf16)


def _pack_kernel(t_ref, o_ref):
    x = t_ref[0]
    w = pltpu.pack_elementwise([x[:, :WORDS], x[:, WORDS:]], packed_dtype=BF16)
    for j in range(SLABS):
        o_ref[pl.ds(j, TM, stride=SLABS), :] = w[:, j * LANES:(j + 1) * LANES]


def _pack_table(tabs, layer):
    return pl.pallas_call(
        _pack_kernel,
        grid=(N_EXPERTS // TM,),
        in_specs=[pl.BlockSpec((1, TM, D_MODEL), lambda i: (layer, i, 0))],
        out_specs=pl.BlockSpec((TM * SLABS, LANES), lambda i: (i, 0)),
        out_shape=jax.ShapeDtypeStruct((N_EXPERTS * SLABS, LANES), jnp.uint32),
        compiler_params=_cparams(1),
        name="peer_pack_table",
    )(tabs)


def _gather_token(idx_ref, t, tab_ref, stage_ref):
    for k in range(N_SEL):
        i = pl.multiple_of(idx_ref.at[k][t], SLABS)
        stage_ref[pl.ds(k * SLABS, SLABS), :] = tab_ref[pl.ds(i, SLABS), :]


def _unpack_words(w, half):
    return pltpu.unpack_elementwise(w, index=half, packed_dtype=BF16, unpacked_dtype=F32)


def _unpack_stage(stage_ref):
    his, los = [], []
    for j in range(SLABS):
        w = stage_ref[pl.ds(j, N_SEL, stride=SLABS), :]
        his.append(_unpack_words(w, 0).astype(BF16))
        los.append(_unpack_words(w, 1).astype(BF16))
    return jnp.concatenate(his + los, axis=1)


def _token_pipeline(idx_hbm, bufs, sems, tab_ref, stages, produce, consume):
    step, n_steps = pl.program_id(0), pl.num_programs(0)
    n_sets = len(stages) // N_STAGE
    sets = [stages[i * N_STAGE:(i + 1) * N_STAGE] for i in range(n_sets)]
    span = n_sets * N_STAGE

    def copy(tile, slot):
        cols = pl.ds(pl.multiple_of(tile * TT, TT), TT)
        return pltpu.make_async_copy(idx_hbm.at[:, cols], bufs[slot], sems.at[slot])

    def gather_group(idx_ref, t0, dst):
        for q in range(N_STAGE):
            _gather_token(idx_ref, t0 + q, tab_ref, dst[q])

    def group(t0, src, refill):
        results = [produce(t0 + q, src[q]) for q in range(N_STAGE)]
        refill()
        consume(t0, results)

    def rotation(idx_ref, t0, refill_first_set):
        for i in range(n_sets - 1):
            t = t0 + i * N_STAGE
            group(t, sets[i], lambda t=t, i=i: gather_group(idx_ref, t + N_STAGE, sets[i + 1]))
        group(t0 + span - N_STAGE, sets[-1], refill_first_set)

    @pl.when(step == 0)
    def _():
        copy(0, 0).start()
        copy(0, 0).wait()
        gather_group(bufs[0], 0, sets[0])

    for slot in range(2):
        @pl.when(lax.rem(step, 2) == slot)
        def _(slot=slot):
            cur, nxt = bufs[slot], bufs[1 - slot]
            has_next = step + 1 < n_steps

            @pl.when(has_next)
            def _():
                copy(step + 1, 1 - slot).start()

            def body(i, carry):
                t0 = i * span
                rotation(cur, t0, lambda: gather_group(cur, t0 + span, sets[0]))
                return carry

            lax.fori_loop(0, TT // span - 1, body, 0)

            @pl.when(has_next)
            def _():
                copy(step + 1, 1 - slot).wait()
                rotation(cur, TT - span, lambda: gather_group(nxt, 0, sets[0]))

            @pl.when(jnp.logical_not(has_next))
            def _():
                rotation(cur, TT - span, lambda: None)


def _peer_u_kernel(idx_hbm, x_ref, tab_ref, act_ref, xb_ref, act_t_ref, idx0, idx1, sems, *stages):
    xb_ref[...] = x_ref[...].astype(BF16).astype(F32)
    act_t_ref[...] = jnp.zeros((N_SEL, TT), F32)
    lane = lax.broadcasted_iota(jnp.int32, (N_SEL, TT), 1)

    def produce(t, stage):
        xr = xb_ref[pl.ds(t, 1), :]
        p = None
        for j in range(SLABS):
            w = stage[pl.ds(j, N_SEL, stride=SLABS), :]
            term = (_unpack_words(w, 0) * xr[:, j * LANES:(j + 1) * LANES]
                    + _unpack_words(w, 1) * xr[:, WORDS + j * LANES:WORDS + (j + 1) * LANES])
            p = term if p is None else p + term
        return jnp.sum(p, axis=1, keepdims=True)

    def consume(t0, cols):
        acc = act_t_ref[...]
        for q, r in enumerate(cols):
            acc = jnp.where(lane == t0 + q, r, acc)
        act_t_ref[...] = acc

    _token_pipeline(idx_hbm, (idx0, idx1), sems, tab_ref, stages, produce, consume)
    act_ref[...] = act_t_ref[...].T


_INDEX_SCRATCH = [pltpu.SMEM((N_SEL, TT), jnp.int32), pltpu.SMEM((N_SEL, TT), jnp.int32),
                  pltpu.SemaphoreType.DMA((2,))]
_STAGE_SCRATCH = [pltpu.VMEM((SLABS * N_SEL, LANES), jnp.uint32)] * N_STAGE


def _peer_u_call(idx, hf, tab):
    t = hf.shape[0]
    return pl.pallas_call(
        _peer_u_kernel,
        grid=(t // TT,),
        in_specs=[
            pl.BlockSpec(memory_space=pl.ANY),
            pl.BlockSpec((TT, D_MODEL), lambda i: (i, 0)),
            pl.BlockSpec(memory_space=pltpu.VMEM),
        ],
        out_specs=pl.BlockSpec((TT, N_SEL), lambda i: (i, 0)),
        out_shape=jax.ShapeDtypeStruct((t, N_SEL), F32),
        scratch_shapes=[pltpu.VMEM((TT, D_MODEL), F32), pltpu.VMEM((N_SEL, TT), F32)]
        + _INDEX_SCRATCH + 2 * _STAGE_SCRATCH,
        compiler_params=_cparams(1),
        name="peer_u",
    )(idx, hf, tab)


def _peer_v_kernel(idx_hbm, act_ref, g_ref, tab_ref, x_ref, gf_ref, lng_ref, lnb_ref, o_ref, w_ref, f_ref,
                   idx0, idx1, sems, *stages):
    w_ref[...] = g_ref[...] * _gelu(act_ref[...])

    def produce(t, stage):
        wr = jnp.broadcast_to(w_ref[pl.ds(t, 1), :], (8, N_SEL)).astype(BF16)
        return jnp.dot(wr, _unpack_stage(stage), preferred_element_type=F32)

    def consume(t0, rows):
        for q, f in enumerate(rows):
            f_ref[pl.ds(t0 + q, 1), :] = f[0:1]

    _token_pipeline(idx_hbm, (idx0, idx1), sems, tab_ref, stages, produce, consume)
    z = DEEPNORM_ALPHA * x_ref[...] + gf_ref[0] * f_ref[...]
    o_ref[...] = _layer_norm(z, lng_ref[...], lnb_ref[...])


def _peer_v_call(idx, act, g, tab, x, gf, lng, lnb, n_prompt_tiles):
    t = x.shape[0]
    sel = lambda i: (jnp.where(i >= n_prompt_tiles, 1, 0), 0, 0)
    row = pl.BlockSpec((1, D_MODEL), lambda i: (0, 0))
    return pl.pallas_call(
        _peer_v_kernel,
        grid=(t // TT,),
        in_specs=[
            pl.BlockSpec(memory_space=pl.ANY),
            pl.BlockSpec((TT, N_SEL), lambda i: (i, 0)),
            pl.BlockSpec((TT, N_SEL), lambda i: (i, 0)),
            pl.BlockSpec(memory_space=pltpu.VMEM),
            pl.BlockSpec((TT, D_MODEL), lambda i: (i, 0)),
            pl.BlockSpec((1, GROUP, D_MODEL), sel),
            row, row,
        ],
        out_specs=pl.BlockSpec((TT, D_MODEL), lambda i: (i, 0)),
        out_shape=jax.ShapeDtypeStruct((t, D_MODEL), F32),
        scratch_shapes=[pltpu.VMEM((TT, N_SEL), F32), pltpu.VMEM((TT, D_MODEL), F32)]
        + _INDEX_SCRATCH + _STAGE_SCRATCH,
        compiler_params=_cparams(1),
        name="peer_v",
    )(idx, act, g, tab, x, gf, lng.reshape(1, D_MODEL), lnb.reshape(1, D_MODEL))


def _time_major(a):
    return jnp.swapaxes(a, 0, 1).reshape(a.shape[0] * a.shape[1], a.shape[2])


def _batch_major(a, nb):
    return jnp.swapaxes(a.reshape(a.shape[0] // nb, nb, a.shape[1]), 0, 1)


def kernel(x_prompt, x_sample, state_lru_h, state_lru_conv, state_conf_conv, c_prompt, c_sample, w_ada, b_ada, ln_mix_g, ln_mix_b, ln_ffn_g, ln_ffn_b, lru_w_in, lru_conv_w, lru_conv_b, lru_w_a, lru_b_a, lru_w_i, lru_b_i, lru_lambda, lru_w_out, conf_w_pw1, conf_b_pw1, conf_dw_w, conf_dw_b, conf_ln_g, conf_ln_b, conf_w_pw2, conf_b_pw2, peer_w_q, peer_keys, peer_u, peer_v):
    bp, lp, _ = x_prompt.shape
    bs, ls, _ = x_sample.shape
    tp, ts = bp * lp, bs * ls
    assert GROUP % bp == 0 and bs == GROUP and TM % GROUP == 0
    for nb, seq in ((bp, lp), (bs, ls)):
        assert (seq * nb) % TM == 0 and (seq * nb == TM or TM // nb >= CONF_CONV_W - 1)
    npt_m, npt_t = tp // TM, tp // TT

    x = jnp.concatenate([_time_major(x_prompt), _time_major(x_sample)], axis=0)

    mod = _mod_call(jnp.concatenate([c_prompt, c_sample], axis=0), w_ada, b_ada)
    mod = mod.reshape(DEPTH, bp + bs, 6, D_MODEL)
    pat = jnp.stack([jnp.tile(mod[:, :bp], (1, GROUP // bp, 1, 1)), mod[:, bp:]], axis=1)
    pat = jnp.transpose(pat, (0, 3, 1, 2, 4))
    mods = [[pat[i, k] for k in range(6)] for i in range(DEPTH)]

    zeros = lambda *s: jnp.zeros(s, F32)
    new_h, new_lconv, new_cconv = [], [], []
    for i in range(DEPTH):
        sh_m, sc_m, g_m, sh_f, sc_f, g_f = mods[i]
        j = i // 2
        if i % 2 == 0:
            proj = _mm_mod_call(x, sc_m, sh_m, lru_w_in[j].astype(BF16), None, npt_m, False, "lru_in_proj")
            wa, wi = lru_w_a[j].astype(BF16), lru_w_i[j].astype(BF16)
            common = (lru_conv_w[j], lru_conv_b[j], wa, lru_b_a[j], wi, lru_b_i[j], lru_lambda[j])
            yp, hp, cp = _lru_call(proj, zeros((LRU_CONV_W - 1) * bp, D_RNN), zeros(bp, D_RNN), *common,
                                   nb=bp, seq=lp, row_block0=0, name="lru_seq_prompt")
            ys, hs, cs = _lru_call(proj, _time_major(state_lru_conv[j]), state_lru_h[j], *common,
                                   nb=bs, seq=ls, row_block0=npt_m, name="lru_seq_sample")
            new_h.append((hp, hs))
            new_lconv.append((_batch_major(cp, bp), _batch_major(cs, bs)))
            mixed, w_out, b_out = (yp, ys), lru_w_out[j], None
        else:
            u = _mm_mod_call(x, sc_m, sh_m, conf_w_pw1[j].astype(BF16), conf_b_pw1[j], npt_m, True, "conf_pw1_glu")
            common = (conf_dw_w[j], conf_dw_b[j], conf_ln_g[j], conf_ln_b[j])
            dp, cp = _conf_call(u, zeros((CONF_CONV_W - 1) * bp, D_CONF), *common,
                                nb=bp, seq=lp, row_block0=0, name="conf_seq_prompt")
            ds, cs = _conf_call(u, _time_major(state_conf_conv[j]), *common,
                                nb=bs, seq=ls, row_block0=npt_m, name="conf_seq_sample")
            new_cconv.append((_batch_major(cp, bp), _batch_major(cs, bs)))
            mixed, w_out, b_out = (dp, ds), conf_w_pw2[j], conf_b_pw2[j]

        x, hf = _proj_ln_call(*mixed, w_out.astype(BF16), b_out, x, g_m, ln_mix_g[i], ln_mix_b[i], sc_f, sh_f,
                              npt_m, "mix_out_ln")
        idx, gate = _retrieve_call(hf, peer_w_q[i].astype(BF16),
                                   peer_keys[i].astype(BF16).reshape(2 * N_PEER_HEADS, N_KEYS, HALF_KEY))
        act = _peer_u_call(idx, hf, _pack_table(peer_u, i))
        x = _peer_v_call(idx, act, gate, _pack_table(peer_v, i), x, g_f, ln_ffn_g[i], ln_ffn_b[i], npt_t)

    y_prompt = _batch_major(x[:tp], bp)
    y_sample = _batch_major(x[tp:], bs)
    pick = lambda pairs, k: jnp.stack([p[k] for p in pairs])
    return (y_prompt, y_sample, pick(new_h, 0), pick(new_lconv, 0), pick(new_cconv, 0),
            pick(new_h, 1), pick(new_lconv, 1), pick(new_cconv, 1))
```

```python
import functools
import math

import jax
import jax.numpy as jnp
from jax import lax
from jax.experimental import pallas as pl
from jax.experimental.pallas import tpu as pltpu

F32 = jnp.float32
BF16 = jnp.bfloat16

D_MODEL = 1024
DEPTH = 2
D_RNN = 1280
N_RNN_HEADS = 10
RNN_HEAD_DIM = 128
LRU_CONV_W = 4
LRU_C = 8.0
D_CONF = 1024
CONF_CONV_W = 31
N_PEER_HEADS = 8
N_KEYS = 128
N_EXPERTS = N_KEYS * N_KEYS
HALF_KEY = 128
TOPK = 16
N_SEL = N_PEER_HEADS * TOPK
DEEPNORM_ALPHA = (2 * DEPTH) ** 0.25
LN_EPS = 1e-5

LANES = 128
GROUP = 128
TM = 512
TT = 128
WORDS = D_MODEL // 2
SLABS = WORDS // LANES
N_STAGE = 4
VMEM_LIMIT = 56 * 1024 * 1024

_NT = (((1,), (1,)), ((), ()))


def _cparams(n_axes, vmem=VMEM_LIMIT):
    return pltpu.CompilerParams(dimension_semantics=("arbitrary",) * n_axes, vmem_limit_bytes=vmem)


def _gelu(x):
    return 0.5 * x * (1.0 + lax.erf(x * (1.0 / math.sqrt(2.0))))


def _affine(x, scale, shift=None):
    rows, c = x.shape
    xr = x.reshape(rows // GROUP, GROUP, c) * scale[None]
    if shift is not None:
        xr = xr + shift[None]
    return xr.reshape(rows, c)


def _layer_norm(z, g, b):
    mu = jnp.mean(z, axis=-1, keepdims=True)
    zc = z - mu
    var = jnp.mean(zc * zc, axis=-1, keepdims=True)
    return zc * lax.rsqrt(var + LN_EPS) * g + b


def _mod_kernel(c_ref, w_ref, b_ref, o_ref):
    c = c_ref[...]
    a = (c * jax.nn.sigmoid(c)).astype(BF16)
    o_ref[0] = jnp.dot(a, w_ref[0].astype(BF16), preferred_element_type=F32) + b_ref[0]


def _mod_call(c_all, w_ada, b_ada):
    nb = c_all.shape[0]
    tn = 1536
    return pl.pallas_call(
        _mod_kernel,
        grid=(DEPTH, 6 * D_MODEL // tn),
        in_specs=[
            pl.BlockSpec((nb, D_MODEL), lambda l, j: (0, 0)),
            pl.BlockSpec((1, D_MODEL, tn), lambda l, j: (l, 0, j)),
            pl.BlockSpec((1, 1, tn), lambda l, j: (l, 0, j)),
        ],
        out_specs=pl.BlockSpec((1, nb, tn), lambda l, j: (l, 0, j)),
        out_shape=jax.ShapeDtypeStruct((DEPTH, nb, 6 * D_MODEL), F32),
        compiler_params=_cparams(2),
        name="adaln_mod",
    )(c_all, w_ada, b_ada.reshape(DEPTH, 1, 6 * D_MODEL))


def _glu_kernel(x_ref, sc_ref, sh_ref, w_ref, b_ref, o_ref):
    xm = _affine(x_ref[...], 1.0 + sc_ref[0], sh_ref[0])
    p = jnp.dot(xm.astype(BF16), w_ref[...], preferred_element_type=F32) + b_ref[...]
    n = p.shape[-1] // 2
    o_ref[...] = p[:, :n] * jax.nn.sigmoid(p[:, n:])


def _glu_call(x, sc, sh, w_bf16, bias, n_prompt_tiles):
    t = x.shape[0]
    n = w_bf16.shape[1]
    sel = lambda i: (jnp.where(i >= n_prompt_tiles, 1, 0), 0, 0)
    return pl.pallas_call(
        _glu_kernel,
        grid=(t // TM,),
        in_specs=[
            pl.BlockSpec((TM, D_MODEL), lambda i: (i, 0)),
            pl.BlockSpec((1, GROUP, D_MODEL), sel),
            pl.BlockSpec((1, GROUP, D_MODEL), sel),
            pl.BlockSpec((D_MODEL, n), lambda i: (0, 0)),
            pl.BlockSpec((1, n), lambda i: (0, 0)),
        ],
        out_specs=pl.BlockSpec((TM, n // 2), lambda i: (i, 0)),
        out_shape=jax.ShapeDtypeStruct((t, n // 2), F32),
        compiler_params=_cparams(1),
        name="conf_pw1_glu",
    )(x, sc, sh, w_bf16, bias.reshape(1, n))


def _lru_kernel(x_ref, sc_ref, sh_ref, win_w_ref, conv0_ref, h0_ref, cw_ref, cb_ref, wa_ref, ba_ref, wi_ref, bi_ref,
                lam_ref, yg_ref, hnew_ref, convnew_ref, win_ref, gb_ref, xc_ref, a_ref, b_ref, h_ref,
                *, nb, tl, cw, n_tiles):
    rows = tl * nb
    halo = (LRU_CONV_W - 1) * nb

    @pl.when(pl.program_id(0) == 0)
    def _():
        win_ref[0:halo, :] = conv0_ref[...]
        h_ref[...] = h0_ref[...]

    xm = _affine(x_ref[...], 1.0 + sc_ref[0], sh_ref[0])
    proj = jnp.dot(xm.astype(BF16), win_w_ref[...], preferred_element_type=F32)
    win_ref[halo:halo + rows, :] = proj[:, :D_RNN]
    gb_ref[...] = proj[:, D_RNN:]
    xc = cb_ref[...] + win_ref[0:rows, :] * cw_ref[0:1, :]
    for w in range(1, LRU_CONV_W):
        xc = xc + win_ref[w * nb:w * nb + rows, :] * cw_ref[w:w + 1, :]
    xc_ref[...] = xc
    carry = win_ref[rows:rows + halo, :]
    convnew_ref[...] = carry
    if n_tiles > 1:
        win_ref[0:halo, :] = carry

    lam = lam_ref[...]
    neg = -lam
    softplus = jnp.maximum(neg, 0.0) + jnp.log1p(jnp.exp(-jnp.abs(neg)))
    for hd in range(N_RNN_HEADS):
        cs = slice(hd * RNN_HEAD_DIM, (hd + 1) * RNN_HEAD_DIM)
        xh = xc_ref[:, cs]
        xb = xh.astype(BF16)
        r = jax.nn.sigmoid(jnp.dot(xb, wa_ref[hd], preferred_element_type=F32) + ba_ref[:, cs])
        ig = jax.nn.sigmoid(jnp.dot(xb, wi_ref[hd], preferred_element_type=F32) + bi_ref[:, cs])
        log_a = (-LRU_C * r) * softplus[:, cs]
        th = jnp.tanh(log_a)
        a_ref[:, cs] = jnp.exp(log_a)
        b_ref[:, cs] = jnp.sqrt(-2.0 * th / (1.0 - th)) * (ig * xh)

    for c0 in range(0, D_RNN, cw):
        cs = slice(c0, c0 + cw)

        def step(l, h, cs=cs):
            rs = pl.ds(pl.multiple_of(l * nb, nb), nb)
            h = a_ref[rs, cs] * h + b_ref[rs, cs]
            b_ref[rs, cs] = h
            return h

        h_ref[:, cs] = lax.fori_loop(0, tl, step, h_ref[:, cs])

    yg_ref[...] = b_ref[...] * _gelu(gb_ref[...])
    hnew_ref[...] = h_ref[...]


def _lru_call(x, sc, sh, w_in, conv0, h0, cw, cb, wa, ba, wi, bi, lam, *, nb, seq, row_block0, trunk, name):
    tl = TM // nb
    n_tiles = seq // tl
    halo = (LRU_CONV_W - 1) * nb
    chan = D_RNN if nb * D_RNN <= 16 * 1024 else LANES
    full = lambda shape: pl.BlockSpec(shape, lambda i: (0,) * len(shape))
    mod = pl.BlockSpec((1, GROUP, D_MODEL), lambda i: (trunk, 0, 0))
    return pl.pallas_call(
        functools.partial(_lru_kernel, nb=nb, tl=tl, cw=chan, n_tiles=n_tiles),
        grid=(n_tiles,),
        in_specs=[
            pl.BlockSpec((TM, D_MODEL), lambda i: (row_block0 + i, 0)),
            mod, mod,
            pl.BlockSpec(memory_space=pltpu.VMEM),
            full((halo, D_RNN)),
            full((nb, D_RNN)),
            full((LRU_CONV_W, D_RNN)),
            full((1, D_RNN)),
            full((N_RNN_HEADS, RNN_HEAD_DIM, RNN_HEAD_DIM)),
            full((1, D_RNN)),
            full((N_RNN_HEADS, RNN_HEAD_DIM, RNN_HEAD_DIM)),
            full((1, D_RNN)),
            full((1, D_RNN)),
        ],
        out_specs=[
            pl.BlockSpec((TM, D_RNN), lambda i: (i, 0)),
            full((nb, D_RNN)),
            full((halo, D_RNN)),
        ],
        out_shape=[
            jax.ShapeDtypeStruct((seq * nb, D_RNN), F32),
            jax.ShapeDtypeStruct((nb, D_RNN), F32),
            jax.ShapeDtypeStruct((halo, D_RNN), F32),
        ],
        scratch_shapes=[
            pltpu.VMEM((halo + TM, D_RNN), F32),
            pltpu.VMEM((TM, D_RNN), F32),
            pltpu.VMEM((TM, D_RNN), F32),
            pltpu.VMEM((TM, D_RNN), F32),
            pltpu.VMEM((TM, D_RNN), F32),
            pltpu.VMEM((nb, D_RNN), F32),
        ],
        compiler_params=_cparams(1),
        name=name,
    )(x, sc, sh, w_in, conv0, h0, cw, cb.reshape(1, D_RNN), wa, ba.reshape(1, D_RNN), wi, bi.reshape(1, D_RNN),
      lam.reshape(1, D_RNN))


def _conf_kernel(u_ref, conv0_ref, dw_ref, db_ref, g_ref, b_ref, o_ref, convnew_ref, win_ref, d_ref,
                 *, nb, tl, n_tiles):
    rows = tl * nb
    halo = (CONF_CONV_W - 1) * nb
    rc = 64
    cc = 256

    @pl.when(pl.program_id(0) == 0)
    def _():
        win_ref[0:halo, :] = conv0_ref[...]

    win_ref[halo:halo + rows, :] = u_ref[...]

    for c0 in range(0, D_CONF, cc):
        cs = slice(c0, c0 + cc)

        def chunk(r, carry, cs=cs):
            r0 = pl.multiple_of(r * rc, rc)
            acc = jnp.broadcast_to(db_ref[:, cs], (rc, cc))
            for w in range(CONF_CONV_W):
                acc = acc + win_ref[pl.ds(r0 + w * nb, rc), cs] * dw_ref[w:w + 1, cs]
            d_ref[pl.ds(r0, rc), cs] = acc
            return carry

        lax.fori_loop(0, rows // rc, chunk, 0)

    convnew_ref[...] = win_ref[rows:rows + halo, :]
    if n_tiles > 1:
        win_ref[0:halo, :] = win_ref[rows:rows + halo, :]

    y = _layer_norm(d_ref[...], g_ref[...], b_ref[...])
    o_ref[...] = y * jax.nn.sigmoid(y)


def _conf_call(u, conv0, dw, db, g, b, *, nb, seq, row_block0, name):
    tl = TM // nb
    n_tiles = seq // tl
    halo = (CONF_CONV_W - 1) * nb
    full = lambda shape: pl.BlockSpec(shape, lambda i: (0,) * len(shape))
    return pl.pallas_call(
        functools.partial(_conf_kernel, nb=nb, tl=tl, n_tiles=n_tiles),
        grid=(n_tiles,),
        in_specs=[
            pl.BlockSpec((TM, D_CONF), lambda i: (row_block0 + i, 0)),
            full((halo, D_CONF)),
            full((CONF_CONV_W, D_CONF)),
            full((1, D_CONF)),
            full((1, D_CONF)),
            full((1, D_CONF)),
        ],
        out_specs=[
            pl.BlockSpec((TM, D_CONF), lambda i: (i, 0)),
            full((halo, D_CONF)),
        ],
        out_shape=[
            jax.ShapeDtypeStruct((seq * nb, D_CONF), F32),
            jax.ShapeDtypeStruct((halo, D_CONF), F32),
        ],
        scratch_shapes=[
            pltpu.VMEM((halo + TM, D_CONF), F32),
            pltpu.VMEM((TM, D_CONF), F32),
        ],
        compiler_params=_cparams(1),
        name=name,
    )(u, conv0, dw, db.reshape(1, D_CONF), g.reshape(1, D_CONF), b.reshape(1, D_CONF))


def _proj_ln_kernel(yp_ref, ys_ref, w_ref, x_ref, gm_ref, lng_ref, lnb_ref, scf_ref, shf_ref, *rest, n_prompt_tiles):
    x1_ref, hf_ref = rest[-2:]
    y = jnp.where(pl.program_id(0) < n_prompt_tiles, yp_ref[...], ys_ref[...])
    out = jnp.dot(y.astype(BF16), w_ref[...], preferred_element_type=F32)
    if len(rest) == 3:
        out = out + rest[0][...]
    z = DEEPNORM_ALPHA * x_ref[...] + _affine(out, gm_ref[0])
    x1 = _layer_norm(z, lng_ref[...], lnb_ref[...])
    x1_ref[...] = x1
    hf_ref[...] = _affine(x1, 1.0 + scf_ref[0], shf_ref[0])


def _proj_ln_call(y_prompt, y_sample, w_bf16, bias, x, gm, lng, lnb, scf, shf, n_prompt_tiles, name):
    t, k = x.shape[0], y_prompt.shape[1]
    sel = lambda i: (jnp.where(i >= n_prompt_tiles, 1, 0), 0, 0)
    row = pl.BlockSpec((1, D_MODEL), lambda i: (0, 0))
    mod = pl.BlockSpec((1, GROUP, D_MODEL), sel)
    tile = pl.BlockSpec((TM, D_MODEL), lambda i: (i, 0))
    in_specs = [
        pl.BlockSpec((TM, k), lambda i: (jnp.minimum(i, n_prompt_tiles - 1), 0)),
        pl.BlockSpec((TM, k), lambda i: (jnp.maximum(i - n_prompt_tiles, 0), 0)),
        pl.BlockSpec((k, D_MODEL), lambda i: (0, 0)),
        tile, mod, row, row, mod, mod,
    ]
    args = [y_prompt, y_sample, w_bf16, x, gm, lng.reshape(1, D_MODEL), lnb.reshape(1, D_MODEL), scf, shf]
    if bias is not None:
        in_specs.append(row)
        args.append(bias.reshape(1, D_MODEL))
    return pl.pallas_call(
        functools.partial(_proj_ln_kernel, n_prompt_tiles=n_prompt_tiles),
        grid=(t // TM,),
        in_specs=in_specs,
        out_specs=[tile, tile],
        out_shape=[jax.ShapeDtypeStruct((t, D_MODEL), F32)] * 2,
        compiler_params=_cparams(1),
        name=name,
    )(*args)


SUB = 8

_CELLS = [(i, j) for i in range(TOPK) for j in range(TOPK) if (i + 1) * (j + 1) <= TOPK]
_CELLS = _CELLS + [None] * (SUB * pl.next_power_of_2(pl.cdiv(len(_CELLS), SUB)) - len(_CELLS))
_CELL_GROUPS = [_CELLS[k:k + SUB] for k in range(0, len(_CELLS), SUB)]


def _stack_rows(rows, r_iota):
    out = jnp.broadcast_to(rows[-1], r_iota.shape)
    for r in range(SUB - 2, -1, -1):
        if rows[r] is not rows[r + 1]:
            out = jnp.where(r_iota <= r, rows[r], out)
    return out


def _row_iota(shape):
    return lax.broadcasted_iota(jnp.int32, shape, 0).astype(F32)


def _sort_network(lo, hi):
    def merge(lo, hi, r):
        step = 2 * r
        if step < hi - lo:
            yield from merge(lo, hi, step)
            yield from merge(lo + r, hi, step)
            yield from ((i, i + r) for i in range(lo + r, hi - r, step))
        else:
            yield (lo, lo + r)

    if hi > lo:
        mid = lo + (hi - lo) // 2
        yield from _sort_network(lo, mid)
        yield from _sort_network(mid + 1, hi)
        yield from merge(lo, hi, 1)


def _top_rows(s, k, tag=None):
    n = s.shape[0]
    depth = n // SUB
    iota = _row_iota(s.shape)
    split = lambda a: [a[v * SUB:(v + 1) * SUB, :] for v in range(depth)]
    cols = [split(-s), split(iota)] + ([] if tag is None else [split(tag)])
    neg, idx = cols[0], cols[1]
    for a, b in _sort_network(0, depth - 1):
        tie = neg[a] == neg[b]
        swap = jnp.where(tie, idx[b], neg[b]) < jnp.where(tie, idx[a], neg[a])
        for c in cols:
            c[a], c[b] = jnp.where(swap, c[b], c[a]), jnp.where(swap, c[a], c[b])
    vals, tags = [], []
    for r in range(k):
        m = jnp.min(neg[0], axis=0, keepdims=True)
        am = jnp.min(jnp.where(neg[0] == m, idx[0], float(n)), axis=0, keepdims=True)
        won = idx[0] == am
        vals.append(-m)
        tags.append(am if tag is None else jnp.max(jnp.where(won, cols[2][0], -1.0), axis=0, keepdims=True))
        for d in range(min(depth - 1, k - 1 - r)):
            for c in cols:
                c[d] = jnp.where(won, c[d + 1], c[d])
    return vals, tags


def _retrieve_kernel(hf_ref, wq_ref, keys_ref, idx_ref, g_ref, g_t_ref):
    q = jnp.dot(hf_ref[...].astype(BF16), wq_ref[...], preferred_element_type=F32).astype(BF16)
    r_iota = lax.broadcasted_iota(jnp.int32, (SUB, TT), 0)
    neg_inf = jnp.full((1, TT), -jnp.inf, F32)
    zero = jnp.zeros((1, TT), F32)
    for h in range(N_PEER_HEADS):
        halves = []
        for p in range(2):
            hp = h * 2 + p
            s = lax.dot_general(keys_ref[hp], q[:, hp * HALF_KEY:(hp + 1) * HALF_KEY], _NT,
                                preferred_element_type=F32)
            halves.append(_top_rows(s, TOPK))
        (v1, i1), (v2, i2) = halves
        i1 = [iv * float(N_KEYS) for iv in i1]
        cand, cidx = [], []
        for cells in _CELL_GROUPS:
            pick = lambda rows, pos, pad: _stack_rows([pad if c is None else rows[c[pos]] for c in cells], r_iota)
            cand.append(pick(v1, 0, neg_inf) + pick(v2, 1, neg_inf))
            cidx.append(pick(i1, 0, zero) + pick(i2, 1, zero))
        cand = jnp.concatenate(cand, axis=0)
        cidx = jnp.concatenate(cidx, axis=0)
        top_s, experts = _top_rows(cand, TOPK, tag=cidx)
        e =[jnp.exp(_stack_rows(top_s[k:k + SUB], r_iota) - top_s[0]) for k in range(0, TOPK, SUB)]
        denom = jnp.sum(sum(e[1:], e[0]), axis=0, keepdims=True)
        for k, ev in enumerate(e):
            rows = slice(h * TOPK + k * SUB, h * TOPK + (k + 1) * SUB)
            g_t_ref[rows, :] = ev / denom
            ids = _stack_rows(experts[k * SUB:(k + 1) * SUB], r_iota)
            idx_ref[rows, :] = ids.astype(jnp.int32) * SLABS
    g_ref[...] = g_t_ref[...].T


def _retrieve_call(hf, wq_bf16, keys_bf16):
    t = hf.shape[0]
    nq = wq_bf16.shape[1]
    return pl.pallas_call(
        _retrieve_kernel,
        grid=(t // TT,),
        in_specs=[
            pl.BlockSpec((TT, D_MODEL), lambda i: (i, 0)),
            pl.BlockSpec((D_MODEL, nq), lambda i: (0, 0)),
            pl.BlockSpec((2 * N_PEER_HEADS, N_KEYS, HALF_KEY), lambda i: (0, 0, 0)),
        ],
        out_specs=[
            pl.BlockSpec((N_SEL, TT), lambda i: (0, i)),
            pl.BlockSpec((TT, N_SEL), lambda i: (i, 0)),
        ],
        out_shape=[
            jax.ShapeDtypeStruct((N_SEL, t), jnp.int32),
            jax.ShapeDtypeStruct((t, N_SEL), F32),
        ],
        scratch_shapes=[pltpu.VMEM((N_SEL, TT), F32)],
        compiler_params=_cparams(1),
        name="peer_retrieve",
    )(hf, wq_bf16, keys_bf16)


def _pack_kernel(t_ref, o_ref):
    x = t_ref[0]
    w = pltpu.pack_elementwise([x[:, :WORDS], x[:, WORDS:]], packed_dtype=BF16)
    for j in range(SLABS):
        o_ref[pl.ds(j, TM, stride=SLABS), :] = w[:, j * LANES:(j + 1) * LANES]


def _pack_table(tabs, layer):
    return pl.pallas_call(
        _pack_kernel,
        grid=(N_EXPERTS // TM,),
        in_specs=[pl.BlockSpec((1, TM, D_MODEL), lambda i: (layer, i, 0))],
        out_specs=pl.BlockSpec((TM * SLABS, LANES), lambda i: (i, 0)),
        out_shape=jax.ShapeDtypeStruct((N_EXPERTS * SLABS, LANES), jnp.uint32),
        compiler_params=_cparams(1),
        name="peer_pack_table",
    )(tabs)


def _gather_token(idx_ref, t, tab_ref, stage_ref):
    for k in range(N_SEL):
        i = pl.multiple_of(idx_ref.at[k][t], SLABS)
        stage_ref[pl.ds(k * SLABS, SLABS), :] = tab_ref[pl.ds(i, SLABS), :]


def _unpack_words(w, half):
    return pltpu.unpack_elementwise(w, index=half, packed_dtype=BF16, unpacked_dtype=F32)


def _unpack_stage(stage_ref):
    his, los = [], []
    for j in range(SLABS):
        w = stage_ref[pl.ds(j, N_SEL, stride=SLABS), :]
        his.append(_unpack_words(w, 0).astype(BF16))
        los.append(_unpack_words(w, 1).astype(BF16))
    return jnp.concatenate(his + los, axis=1)


def _token_pipeline(idx_hbm, bufs, sems, tab_ref, stages, produce, consume):
    step, n_steps = pl.program_id(0), pl.num_programs(0)
    n_groups = TT // N_STAGE

    def copy(tile, slot):
        cols = pl.ds(pl.multiple_of(tile * TT, TT), TT)
        return pltpu.make_async_copy(idx_hbm.at[:, cols], bufs[slot], sems.at[slot])

    def gather_group(idx_ref, t0):
        for q in range(N_STAGE):
            _gather_token(idx_ref, t0 + q, tab_ref, stages[q])

    def group(t0, refill):
        results = [produce(t0 + q, stages[q]) for q in range(N_STAGE)]
        refill()
        consume(t0, results)

    @pl.when(step == 0)
    def _():
        copy(0, 0).start()
        copy(0, 0).wait()
        gather_group(bufs[0], 0)

    for slot in range(2):
        @pl.when(lax.rem(step, 2) == slot)
        def _(slot=slot):
            has_next = step + 1 < n_steps

            @pl.when(has_next)
            def _():
                copy(step + 1, 1 - slot).start()

            def body(i, carry):
                t0 = i * N_STAGE
                group(t0, lambda: gather_group(bufs[slot], t0 + N_STAGE))
                return carry

            lax.fori_loop(0, n_groups - 1, body, 0)

            @pl.when(has_next)
            def _():
                copy(step + 1, 1 - slot).wait()
                group(TT - N_STAGE, lambda: gather_group(bufs[1 - slot], 0))

            @pl.when(jnp.logical_not(has_next))
            def _():
                group(TT - N_STAGE, lambda: None)


def _peer_u_kernel(idx_hbm, x_ref, tab_ref, act_ref, xb_ref, act_t_ref, idx0, idx1, sems, *stages):
    xb_ref[...] = x_ref[...].astype(BF16).astype(F32)
    act_t_ref[...] = jnp.zeros((N_SEL, TT), F32)
    lane = lax.broadcasted_iota(jnp.int32, (N_SEL, TT), 1)

    def produce(t, stage):
        xr = xb_ref[pl.ds(t, 1), :]
        p = None
        for j in range(SLABS):
            w = stage[pl.ds(j, N_SEL, stride=SLABS), :]
            term = (_unpack_words(w, 0) * xr[:, j * LANES:(j + 1) * LANES]
                    + _unpack_words(w, 1) * xr[:, WORDS + j * LANES:WORDS + (j + 1) * LANES])
            p = term if p is None else p + term
        return jnp.sum(p, axis=1, keepdims=True)

    def consume(t0, cols):
        acc = act_t_ref[...]
        for q, r in enumerate(cols):
            acc = jnp.where(lane == t0 + q, r, acc)
        act_t_ref[...] = acc

    _token_pipeline(idx_hbm, (idx0, idx1), sems, tab_ref, stages, produce, consume)
    act_ref[...] = act_t_ref[...].T


_INDEX_SCRATCH = [pltpu.SMEM((N_SEL, TT), jnp.int32), pltpu.SMEM((N_SEL, TT), jnp.int32),
                  pltpu.SemaphoreType.DMA((2,))]
_STAGE_SCRATCH = [pltpu.VMEM((SLABS * N_SEL, LANES), jnp.uint32)] * N_STAGE


def _peer_u_call(idx, hf, tab):
    t = hf.shape[0]
    return pl.pallas_call(
        _peer_u_kernel,
        grid=(t // TT,),
        in_specs=[
            pl.BlockSpec(memory_space=pl.ANY),
            pl.BlockSpec((TT, D_MODEL), lambda i: (i, 0)),
            pl.BlockSpec(memory_space=pltpu.VMEM),
        ],
        out_specs=pl.BlockSpec((TT, N_SEL), lambda i: (i, 0)),
        out_shape=jax.ShapeDtypeStruct((t, N_SEL), F32),
        scratch_shapes=[pltpu.VMEM((TT, D_MODEL), F32), pltpu.VMEM((N_SEL, TT), F32)]
        + _INDEX_SCRATCH + _STAGE_SCRATCH,
        compiler_params=_cparams(1),
        name="peer_u",
    )(idx, hf, tab)


def _peer_v_kernel(idx_hbm, act_ref, g_ref, tab_ref, x_ref, gf_ref, lng_ref, lnb_ref, o_ref, w_ref, f_ref,
                   idx0, idx1, sems, *stages):
    w_ref[...] = g_ref[...] * _gelu(act_ref[...])

    def produce(t, stage):
        wr = jnp.broadcast_to(w_ref[pl.ds(t, 1), :], (8, N_SEL)).astype(BF16)
        return jnp.dot(wr, _unpack_stage(stage), preferred_element_type=F32)

    def consume(t0, rows):
        for q, f in enumerate(rows):
            f_ref[pl.ds(t0 + q, 1), :] = f[0:1]

    _token_pipeline(idx_hbm, (idx0, idx1), sems, tab_ref, stages, produce, consume)
    z = DEEPNORM_ALPHA * x_ref[...] + gf_ref[0] * f_ref[...]
    o_ref[...] = _layer_norm(z, lng_ref[...], lnb_ref[...])


def _peer_v_call(idx, act, g, tab, x, gf, lng, lnb, n_prompt_tiles):
    t = x.shape[0]
    sel = lambda i: (jnp.where(i >= n_prompt_tiles, 1, 0), 0, 0)
    row = pl.BlockSpec((1, D_MODEL), lambda i: (0, 0))
    return pl.pallas_call(
        _peer_v_kernel,
        grid=(t // TT,),
        in_specs=[
            pl.BlockSpec(memory_space=pl.ANY),
            pl.BlockSpec((TT, N_SEL), lambda i: (i, 0)),
            pl.BlockSpec((TT, N_SEL), lambda i: (i, 0)),
            pl.BlockSpec(memory_space=pltpu.VMEM),
            pl.BlockSpec((TT, D_MODEL), lambda i: (i, 0)),
            pl.BlockSpec((1, GROUP, D_MODEL), sel),
            row, row,
        ],
        out_specs=pl.BlockSpec((TT, D_MODEL), lambda i: (i, 0)),
        out_shape=jax.ShapeDtypeStruct((t, D_MODEL), F32),
        scratch_shapes=[pltpu.VMEM((TT, N_SEL), F32), pltpu.VMEM((TT, D_MODEL), F32)]
        + _INDEX_SCRATCH + _STAGE_SCRATCH,
        compiler_params=_cparams(1),
        name="peer_v",
    )(idx, act, g, tab, x, gf, lng.reshape(1, D_MODEL), lnb.reshape(1, D_MODEL))


def _time_major(a):
    return jnp.swapaxes(a, 0, 1).reshape(a.shape[0] * a.shape[1], a.shape[2])


def _batch_major(a, nb):
    return jnp.swapaxes(a.reshape(a.shape[0] // nb, nb, a.shape[1]), 0, 1)


def kernel(x_prompt, x_sample, state_lru_h, state_lru_conv, state_conf_conv, c_prompt, c_sample, w_ada, b_ada, ln_mix_g, ln_mix_b, ln_ffn_g, ln_ffn_b, lru_w_in, lru_conv_w, lru_conv_b, lru_w_a, lru_b_a, lru_w_i, lru_b_i, lru_lambda, lru_w_out, conf_w_pw1, conf_b_pw1, conf_dw_w, conf_dw_b, conf_ln_g, conf_ln_b, conf_w_pw2, conf_b_pw2, peer_w_q, peer_keys, peer_u, peer_v):
    bp, lp, _ = x_prompt.shape
    bs, ls, _ = x_sample.shape
    tp, ts = bp * lp, bs * ls
    assert GROUP % bp == 0 and bs == GROUP and TM % GROUP == 0
    for nb, seq in ((bp, lp), (bs, ls)):
        assert (seq * nb) % TM == 0 and (seq * nb == TM or TM // nb >= CONF_CONV_W - 1)
    npt_m, npt_t = tp // TM, tp // TT

    x = jnp.concatenate([_time_major(x_prompt), _time_major(x_sample)], axis=0)

    mod = _mod_call(jnp.concatenate([c_prompt, c_sample], axis=0), w_ada, b_ada)
    mod = mod.reshape(DEPTH, bp + bs, 6, D_MODEL)
    pat = jnp.stack([jnp.tile(mod[:, :bp], (1, GROUP // bp, 1, 1)), mod[:, bp:]], axis=1)
    pat = jnp.transpose(pat, (0, 3, 1, 2, 4))
    mods = [[pat[i, k] for k in range(6)] for i in range(DEPTH)]

    zeros = lambda *s: jnp.zeros(s, F32)
    new_h, new_lconv, new_cconv = [], [], []
    for i in range(DEPTH):
        sh_m, sc_m, g_m, sh_f, sc_f, g_f = mods[i]
        j = i // 2
        if i % 2 == 0:
            wa, wi = lru_w_a[j].astype(BF16), lru_w_i[j].astype(BF16)
            head = (x, sc_m, sh_m, lru_w_in[j].astype(BF16))
            common = (lru_conv_w[j], lru_conv_b[j], wa, lru_b_a[j], wi, lru_b_i[j], lru_lambda[j])
            yp, hp, cp = _lru_call(*head, zeros((LRU_CONV_W - 1) * bp, D_RNN), zeros(bp, D_RNN), *common,
                                   nb=bp, seq=lp, row_block0=0, trunk=0, name="lru_seq_prompt")
            ys, hs, cs = _lru_call(*head, _time_major(state_lru_conv[j]), state_lru_h[j], *common,
                                   nb=bs, seq=ls, row_block0=npt_m, trunk=1, name="lru_seq_sample")
            new_h.append((hp, hs))
            new_lconv.append((_batch_major(cp, bp), _batch_major(cs, bs)))
            mixed, w_out, b_out = (yp, ys), lru_w_out[j], None
        else:
            u = _glu_call(x, sc_m, sh_m, conf_w_pw1[j].astype(BF16), conf_b_pw1[j], npt_m)
            common = (conf_dw_w[j], conf_dw_b[j], conf_ln_g[j], conf_ln_b[j])
            dp, cp = _conf_call(u, zeros((CONF_CONV_W - 1) * bp, D_CONF), *common,
                                nb=bp, seq=lp, row_block0=0, name="conf_seq_prompt")
            ds, cs = _conf_call(u, _time_major(state_conf_conv[j]), *common,
                                nb=bs, seq=ls, row_block0=npt_m, name="conf_seq_sample")
            new_cconv.append((_batch_major(cp, bp), _batch_major(cs, bs)))
            mixed, w_out, b_out = (dp, ds), conf_w_pw2[j], conf_b_pw2[j]

        x, hf = _proj_ln_call(*mixed, w_out.astype(BF16), b_out, x, g_m, ln_mix_g[i], ln_mix_b[i], sc_f, sh_f,
                              npt_m, "mix_out_ln")
        idx, gate = _retrieve_call(hf, peer_w_q[i].astype(BF16),
                                   peer_keys[i].astype(BF16).reshape(2 * N_PEER_HEADS, N_KEYS, HALF_KEY))
        act = _peer_u_call(idx, hf, _pack_table(peer_u, i))
        x = _peer_v_call(idx, act, gate, _pack_table(peer_v, i), x, g_f, ln_ffn_g[i], ln_ffn_b[i], npt_t)

    y_prompt = _batch_major(x[:tp], bp)
    y_sample = _batch_major(x[tp:], bs)
    pick = lambda pairs, k: jnp.stack([p[k] for p in pairs])
    return (y_prompt, y_sample, pick(new_h, 0), pick(new_lconv, 0), pick(new_cconv, 0),
            pick(new_h, 1), pick(new_lconv, 1), pick(new_cconv, 1))
```

```python
import functools
import math

import jax
import jax.numpy as jnp
from jax import lax
from jax.experimental import pallas as pl
from jax.experimental.pallas import tpu as pltpu

F32 = jnp.float32
BF16 = jnp.bfloat16

D_MODEL = 1024
DEPTH = 2
D_RNN = 1280
N_RNN_HEADS = 10
RNN_HEAD_DIM = 128
LRU_CONV_W = 4
LRU_C = 8.0
D_CONF = 1024
CONF_CONV_W = 31
N_PEER_HEADS = 8
N_KEYS = 128
N_EXPERTS = N_KEYS * N_KEYS
HALF_KEY = 128
TOPK = 16
N_SEL = N_PEER_HEADS * TOPK
DEEPNORM_ALPHA = (2 * DEPTH) ** 0.25
LN_EPS = 1e-5

LANES = 128
GROUP = 128
TM = 512
TT = 128
TE = 2 * TT
WORDS = D_MODEL // 2
SLABS = WORDS // LANES
N_STAGE = 4
VMEM_LIMIT = 56 * 1024 * 1024

_NT = (((1,), (1,)), ((), ()))


def _cparams(n_axes, vmem=VMEM_LIMIT):
    return pltpu.CompilerParams(dimension_semantics=("arbitrary",) * n_axes, vmem_limit_bytes=vmem)


def _gelu(x):
    return 0.5 * x * (1.0 + lax.erf(x * (1.0 / math.sqrt(2.0))))


def _affine(x, scale, shift=None):
    rows, c = x.shape
    xr = x.reshape(rows // GROUP, GROUP, c) * scale[None]
    if shift is not None:
        xr = xr + shift[None]
    return xr.reshape(rows, c)


def _layer_norm(z, g, b):
    mu = jnp.mean(z, axis=-1, keepdims=True)
    zc = z - mu
    var = jnp.mean(zc * zc, axis=-1, keepdims=True)
    return zc * lax.rsqrt(var + LN_EPS) * g + b


def _mod_kernel(c_ref, w_ref, b_ref, o_ref):
    c = c_ref[...]
    a = (c * jax.nn.sigmoid(c)).astype(BF16)
    o_ref[0] = jnp.dot(a, w_ref[0].astype(BF16), preferred_element_type=F32) + b_ref[0]


def _mod_call(c_all, w_ada, b_ada):
    nb = c_all.shape[0]
    tn = 1536
    return pl.pallas_call(
        _mod_kernel,
        grid=(DEPTH, 6 * D_MODEL // tn),
        in_specs=[
            pl.BlockSpec((nb, D_MODEL), lambda l, j: (0, 0)),
            pl.BlockSpec((1, D_MODEL, tn), lambda l, j: (l, 0, j)),
            pl.BlockSpec((1, 1, tn), lambda l, j: (l, 0, j)),
        ],
        out_specs=pl.BlockSpec((1, nb, tn), lambda l, j: (l, 0, j)),
        out_shape=jax.ShapeDtypeStruct((DEPTH, nb, 6 * D_MODEL), F32),
        compiler_params=_cparams(2),
        name="adaln_mod",
    )(c_all, w_ada, b_ada.reshape(DEPTH, 1, 6 * D_MODEL))


def _mm_mod_kernel(x_ref, sc_ref, sh_ref, w_ref, *rest, glu):
    o_ref = rest[-1]
    xm = _affine(x_ref[...], 1.0 + sc_ref[0], sh_ref[0])
    p = jnp.dot(xm.astype(BF16), w_ref[...], preferred_element_type=F32)
    if len(rest) == 2:
        p = p + rest[0][...]
    if glu:
        n = p.shape[-1] // 2
        p = p[:, :n] * jax.nn.sigmoid(p[:, n:])
    o_ref[...] = p


def _mm_mod_call(x, sc, sh, w_bf16, bias, n_prompt_tiles, glu, name):
    t = x.shape[0]
    n = w_bf16.shape[1]
    n_out = n // 2 if glu else n
    sel = lambda i: (jnp.where(i >= n_prompt_tiles, 1, 0), 0, 0)
    in_specs = [
        pl.BlockSpec((TM, D_MODEL), lambda i: (i, 0)),
        pl.BlockSpec((1, GROUP, D_MODEL), sel),
        pl.BlockSpec((1, GROUP, D_MODEL), sel),
        pl.BlockSpec((D_MODEL, n), lambda i: (0, 0)),
    ]
    args = [x, sc, sh, w_bf16]
    if bias is not None:
        in_specs.append(pl.BlockSpec((1, n), lambda i: (0, 0)))
        args.append(bias.reshape(1, n))
    return pl.pallas_call(
        functools.partial(_mm_mod_kernel, glu=glu),
        grid=(t // TM,),
        in_specs=in_specs,
        out_specs=pl.BlockSpec((TM, n_out), lambda i: (i, 0)),
        out_shape=jax.ShapeDtypeStruct((t, n_out), F32),
        compiler_params=_cparams(1),
        name=name,
    )(*args)


def _lru_kernel(proj_ref, conv0_ref, h0_ref, cw_ref, cb_ref, wa_ref, ba_ref, wi_ref, bi_ref, lam_ref,
                yg_ref, hnew_ref, convnew_ref, win_ref, xc_ref, a_ref, b_ref, h_ref, *, nb, tl, cw, n_tiles):
    rows = tl * nb
    halo = (LRU_CONV_W - 1) * nb

    @pl.when(pl.program_id(0) == 0)
    def _():
        win_ref[0:halo, :] = conv0_ref[...]
        h_ref[...] = h0_ref[...]

    win_ref[halo:halo + rows, :] = proj_ref[:, :D_RNN]
    xc = cb_ref[...] + win_ref[0:rows, :] * cw_ref[0:1, :]
    for w in range(1, LRU_CONV_W):
        xc = xc + win_ref[w * nb:w * nb + rows, :] * cw_ref[w:w + 1, :]
    xc_ref[...] = xc
    carry = win_ref[rows:rows + halo, :]
    convnew_ref[...] = carry
    if n_tiles > 1:
        win_ref[0:halo, :] = carry

    lam = lam_ref[...]
    neg = -lam
    softplus = jnp.maximum(neg, 0.0) + jnp.log1p(jnp.exp(-jnp.abs(neg)))
    for hd in range(N_RNN_HEADS):
        cs = slice(hd * RNN_HEAD_DIM, (hd + 1) * RNN_HEAD_DIM)
        xh = xc_ref[:, cs]
        xb = xh.astype(BF16)
        r = jax.nn.sigmoid(jnp.dot(xb, wa_ref[hd], preferred_element_type=F32) + ba_ref[:, cs])
        ig = jax.nn.sigmoid(jnp.dot(xb, wi_ref[hd], preferred_element_type=F32) + bi_ref[:, cs])
        log_a = (-LRU_C * r) * softplus[:, cs]
        th = jnp.tanh(log_a)
        a_ref[:, cs] = jnp.exp(log_a)
        b_ref[:, cs] = jnp.sqrt(-2.0 * th / (1.0 - th)) * (ig * xh)

    for c0 in range(0, D_RNN, cw):
        cs = slice(c0, c0 + cw)

        def step(l, h, cs=cs):
            rs = pl.ds(pl.multiple_of(l * nb, nb), nb)
            h = a_ref[rs, cs] * h + b_ref[rs, cs]
            b_ref[rs, cs] = h
            return h

        h_ref[:, cs] = lax.fori_loop(0, tl, step, h_ref[:, cs])

    yg_ref[...] = b_ref[...] * _gelu(proj_ref[:, D_RNN:])
    hnew_ref[...] = h_ref[...]


def _lru_call(proj, conv0, h0, cw, cb, wa, ba, wi, bi, lam, *, nb, seq, row_block0, name):
    tl = TM // nb
    n_tiles = seq // tl
    halo = (LRU_CONV_W - 1) * nb
    chan = D_RNN if nb * D_RNN <= 16 * 1024 else LANES
    full = lambda shape: pl.BlockSpec(shape, lambda i: (0,) * len(shape))
    return pl.pallas_call(
        functools.partial(_lru_kernel, nb=nb, tl=tl, cw=chan, n_tiles=n_tiles),
        grid=(n_tiles,),
        in_specs=[
            pl.BlockSpec((TM, 2 * D_RNN), lambda i: (row_block0 + i, 0)),
            full((halo, D_RNN)),
            full((nb, D_RNN)),
            full((LRU_CONV_W, D_RNN)),
            full((1, D_RNN)),
            full((N_RNN_HEADS, RNN_HEAD_DIM, RNN_HEAD_DIM)),
            full((1, D_RNN)),
            full((N_RNN_HEADS, RNN_HEAD_DIM, RNN_HEAD_DIM)),
            full((1, D_RNN)),
            full((1, D_RNN)),
        ],
        out_specs=[
            pl.BlockSpec((TM, D_RNN), lambda i: (i, 0)),
            full((nb, D_RNN)),
            full((halo, D_RNN)),
        ],
        out_shape=[
            jax.ShapeDtypeStruct((seq * nb, D_RNN), F32),
            jax.ShapeDtypeStruct((nb, D_RNN), F32),
            jax.ShapeDtypeStruct((halo, D_RNN), F32),
        ],
        scratch_shapes=[
            pltpu.VMEM((halo + TM, D_RNN), F32),
            pltpu.VMEM((TM, D_RNN), F32),
            pltpu.VMEM((TM, D_RNN), F32),
            pltpu.VMEM((TM, D_RNN), F32),
            pltpu.VMEM((nb, D_RNN), F32),
        ],
        compiler_params=_cparams(1),
        name=name,
    )(proj, conv0, h0, cw, cb.reshape(1, D_RNN), wa, ba.reshape(1, D_RNN), wi, bi.reshape(1, D_RNN),
      lam.reshape(1, D_RNN))


def _conf_kernel(u_ref, conv0_ref, dw_ref, db_ref, g_ref, b_ref, o_ref, convnew_ref, win_ref, d_ref,
                 *, nb, tl, n_tiles):
    rows = tl * nb
    halo = (CONF_CONV_W - 1) * nb
    rc = 64
    cc = 256

    @pl.when(pl.program_id(0) == 0)
    def _():
        win_ref[0:halo, :] = conv0_ref[...]

    win_ref[halo:halo + rows, :] = u_ref[...]

    for c0 in range(0, D_CONF, cc):
        cs = slice(c0, c0 + cc)

        def chunk(r, carry, cs=cs):
            r0 = pl.multiple_of(r * rc, rc)
            acc = jnp.broadcast_to(db_ref[:, cs], (rc, cc))
            for w in range(CONF_CONV_W):
                acc = acc + win_ref[pl.ds(r0 + w * nb, rc), cs] * dw_ref[w:w + 1, cs]
            d_ref[pl.ds(r0, rc), cs] = acc
            return carry

        lax.fori_loop(0, rows // rc, chunk, 0)

    convnew_ref[...] = win_ref[rows:rows + halo, :]
    if n_tiles > 1:
        win_ref[0:halo, :] = win_ref[rows:rows + halo, :]

    y = _layer_norm(d_ref[...], g_ref[...], b_ref[...])
    o_ref[...] = y * jax.nn.sigmoid(y)


def _conf_call(u, conv0, dw, db, g, b, *, nb, seq, row_block0, name):
    tl = TM // nb
    n_tiles = seq // tl
    halo = (CONF_CONV_W - 1) * nb
    full = lambda shape: pl.BlockSpec(shape, lambda i: (0,) * len(shape))
    return pl.pallas_call(
        functools.partial(_conf_kernel, nb=nb, tl=tl, n_tiles=n_tiles),
        grid=(n_tiles,),
        in_specs=[
            pl.BlockSpec((TM, D_CONF), lambda i: (row_block0 + i, 0)),
            full((halo, D_CONF)),
            full((CONF_CONV_W, D_CONF)),
            full((1, D_CONF)),
            full((1, D_CONF)),
            full((1, D_CONF)),
        ],
        out_specs=[
            pl.BlockSpec((TM, D_CONF), lambda i: (i, 0)),
            full((halo, D_CONF)),
        ],
        out_shape=[
            jax.ShapeDtypeStruct((seq * nb, D_CONF), F32),
            jax.ShapeDtypeStruct((halo, D_CONF), F32),
        ],
        scratch_shapes=[
            pltpu.VMEM((halo + TM, D_CONF), F32),
            pltpu.VMEM((TM, D_CONF), F32),
        ],
        compiler_params=_cparams(1),
        name=name,
    )(u, conv0, dw, db.reshape(1, D_CONF), g.reshape(1, D_CONF), b.reshape(1, D_CONF))


def _proj_ln_kernel(yp_ref, ys_ref, w_ref, x_ref, gm_ref, lng_ref, lnb_ref, scf_ref, shf_ref, *rest, n_prompt_tiles):
    x1_ref, hf_ref = rest[-2:]
    y = jnp.where(pl.program_id(0) < n_prompt_tiles, yp_ref[...], ys_ref[...])
    out = jnp.dot(y.astype(BF16), w_ref[...], preferred_element_type=F32)
    if len(rest) == 3:
        out = out + rest[0][...]
    z = DEEPNORM_ALPHA * x_ref[...] + _affine(out, gm_ref[0])
    x1 = _layer_norm(z, lng_ref[...], lnb_ref[...])
    x1_ref[...] = x1
    hf_ref[...] = _affine(x1, 1.0 + scf_ref[0], shf_ref[0])


def _proj_ln_call(y_prompt, y_sample, w_bf16, bias, x, gm, lng, lnb, scf, shf, n_prompt_tiles, name):
    t, k = x.shape[0], y_prompt.shape[1]
    sel = lambda i: (jnp.where(i >= n_prompt_tiles, 1, 0), 0, 0)
    row = pl.BlockSpec((1, D_MODEL), lambda i: (0, 0))
    mod = pl.BlockSpec((1, GROUP, D_MODEL), sel)
    tile = pl.BlockSpec((TM, D_MODEL), lambda i: (i, 0))
    in_specs = [
        pl.BlockSpec((TM, k), lambda i: (jnp.minimum(i, n_prompt_tiles - 1), 0)),
        pl.BlockSpec((TM, k), lambda i: (jnp.maximum(i - n_prompt_tiles, 0), 0)),
        pl.BlockSpec((k, D_MODEL), lambda i: (0, 0)),
        tile, mod, row, row, mod, mod,
    ]
    args = [y_prompt, y_sample, w_bf16, x, gm, lng.reshape(1, D_MODEL), lnb.reshape(1, D_MODEL), scf, shf]
    if bias is not None:
        in_specs.append(row)
        args.append(bias.reshape(1, D_MODEL))
    return pl.pallas_call(
        functools.partial(_proj_ln_kernel, n_prompt_tiles=n_prompt_tiles),
        grid=(t // TM,),
        in_specs=in_specs,
        out_specs=[tile, tile],
        out_shape=[jax.ShapeDtypeStruct((t, D_MODEL), F32)] * 2,
        compiler_params=_cparams(1),
        name=name,
    )(*args)


SUB = 8

_CELLS = [(i, j) for i in range(TOPK) for j in range(TOPK) if (i + 1) * (j + 1) <= TOPK]
_CELLS = _CELLS + [None] * (SUB * pl.next_power_of_2(pl.cdiv(len(_CELLS), SUB)) - len(_CELLS))
_CELL_GROUPS = [_CELLS[k:k + SUB] for k in range(0, len(_CELLS), SUB)]


def _stack_rows(rows, r_iota):
    out = jnp.broadcast_to(rows[-1], r_iota.shape)
    for r in range(SUB - 2, -1, -1):
        if rows[r] is not rows[r + 1]:
            out = jnp.where(r_iota <= r, rows[r], out)
    return out


def _row_iota(shape):
    return lax.broadcasted_iota(jnp.int32, shape, 0).astype(F32)


def _sort_network(lo, hi):
    def merge(lo, hi, r):
        step = 2 * r
        if step < hi - lo:
            yield from merge(lo, hi, step)
            yield from merge(lo + r, hi, step)
            yield from ((i, i + r) for i in range(lo + r, hi - r, step))
        else:
            yield (lo, lo + r)

    if hi > lo:
        mid = lo + (hi - lo) // 2
        yield from _sort_network(lo, mid)
        yield from _sort_network(mid + 1, hi)
        yield from merge(lo, hi, 1)


def _top_rows(s, k, tag=None):
    n = s.shape[0]
    depth = n // SUB
    iota = _row_iota(s.shape)
    split = lambda a: [a[v * SUB:(v + 1) * SUB, :] for v in range(depth)]
    cols = [split(-s), split(iota)] + ([] if tag is None else [split(tag)])
    neg, idx = cols[0], cols[1]
    for a, b in _sort_network(0, depth - 1):
        tie = neg[a] == neg[b]
        swap = jnp.where(tie, idx[b], neg[b]) < jnp.where(tie, idx[a], neg[a])
        for c in cols:
            c[a], c[b] = jnp.where(swap, c[b], c[a]), jnp.where(swap, c[a], c[b])
    vals, tags = [], []
    for r in range(k):
        m = jnp.min(neg[0], axis=0, keepdims=True)
        am = jnp.min(jnp.where(neg[0] == m, idx[0], float(n)), axis=0, keepdims=True)
        won = idx[0] == am
        vals.append(-m)
        tags.append(am if tag is None else jnp.max(jnp.where(won, cols[2][0], -1.0), axis=0, keepdims=True))
        for d in range(min(depth - 1, k - 1 - r)):
            for c in cols:
                c[d] = jnp.where(won, c[d + 1], c[d])
    return vals, tags


def _retrieve_kernel(hf_ref, wq_ref, keys_ref, idx_ref, g_ref, g_t_ref):
    q = jnp.dot(hf_ref[...].astype(BF16), wq_ref[...], preferred_element_type=F32).astype(BF16)
    r_iota = lax.broadcasted_iota(jnp.int32, (SUB, TT), 0)
    neg_inf = jnp.full((1, TT), -jnp.inf, F32)
    zero = jnp.zeros((1, TT), F32)
    for h in range(N_PEER_HEADS):
        halves = []
        for p in range(2):
            hp = h * 2 + p
            s = lax.dot_general(keys_ref[hp], q[:, hp * HALF_KEY:(hp + 1) * HALF_KEY], _NT,
                                preferred_element_type=F32)
            halves.append(_top_rows(s, TOPK))
        (v1, i1), (v2, i2) = halves
        i1 = [iv * float(N_KEYS) for iv in i1]
        cand, cidx = [], []
        for cells in _CELL_GROUPS:
            pick = lambda rows, pos, pad: _stack_rows([pad if c is None else rows[c[pos]] for c in cells], r_iota)
            cand.append(pick(v1, 0, neg_inf) + pick(v2, 1, neg_inf))
            cidx.append(pick(i1, 0, zero) + pick(i2, 1, zero))
        cand = jnp.concatenate(cand, axis=0)
        cidx = jnp.concatenate(cidx, axis=0)
        top_s, experts = _top_rows(cand, TOPK, tag=cidx)
        e =[jnp.exp(_stack_rows(top_s[k:k + SUB], r_iota) - top_s[0]) for k in range(0, TOPK, SUB)]
        denom = jnp.sum(sum(e[1:], e[0]), axis=0, keepdims=True)
        for k, ev in enumerate(e):
            rows = slice(h * TOPK + k * SUB, h * TOPK + (k + 1) * SUB)
            g_t_ref[rows, :] = ev / denom
            ids = _stack_rows(experts[k * SUB:(k + 1) * SUB], r_iota)
            idx_ref[rows, :] = ids.astype(jnp.int32) * SLABS
    g_ref[...] = g_t_ref[...].T


def _retrieve_call(hf, wq_bf16, keys_bf16):
    t = hf.shape[0]
    nq = wq_bf16.shape[1]
    return pl.pallas_call(
        _retrieve_kernel,
        grid=(t // TT,),
        in_specs=[
            pl.BlockSpec((TT, D_MODEL), lambda i: (i, 0)),
            pl.BlockSpec((D_MODEL, nq), lambda i: (0, 0)),
            pl.BlockSpec((2 * N_PEER_HEADS, N_KEYS, HALF_KEY), lambda i: (0, 0, 0)),
        ],
        out_specs=[
            pl.BlockSpec((N_SEL, TT), lambda i: (0, i)),
            pl.BlockSpec((TT, N_SEL), lambda i: (i, 0)),
        ],
        out_shape=[
            jax.ShapeDtypeStruct((N_SEL, t), jnp.int32),
            jax.ShapeDtypeStruct((t, N_SEL), F32),
        ],
        scratch_shapes=[pltpu.VMEM((N_SEL, TT), F32)],
        compiler_params=_cparams(1),
        name="peer_retrieve",
    )(hf, wq_bf16, keys_bf16)


def _pack_kernel(t_ref, o_ref):
    x = t_ref[0]
    w = pltpu.pack_elementwise([x[:, :WORDS], x[:, WORDS:]], packed_dtype=BF16)
    for j in range(SLABS):
        o_ref[pl.ds(j, TM, stride=SLABS), :] = w[:, j * LANES:(j + 1) * LANES]


def _pack_table(tabs, layer):
    return pl.pallas_call(
        _pack_kernel,
        grid=(N_EXPERTS // TM,),
        in_specs=[pl.BlockSpec((1, TM, D_MODEL), lambda i: (layer, i, 0))],
        out_specs=pl.BlockSpec((TM * SLABS, LANES), lambda i: (i, 0)),
        out_shape=jax.ShapeDtypeStruct((N_EXPERTS * SLABS, LANES), jnp.uint32),
        compiler_params=_cparams(1),
        name="peer_pack_table",
    )(tabs)


def _gather_token(idx_ref, t, tab_ref, stage_ref):
    for k in range(N_SEL):
        i = pl.multiple_of(idx_ref.at[k][t], SLABS)
        stage_ref[pl.ds(k * SLABS, SLABS), :] = tab_ref[pl.ds(i, SLABS), :]


def _unpack_words(w, half):
    return pltpu.unpack_elementwise(w, index=half, packed_dtype=BF16, unpacked_dtype=F32)


def _unpack_stage(stage_ref):
    his, los = [], []
    for j in range(SLABS):
        w = stage_ref[pl.ds(j, N_SEL, stride=SLABS), :]
        his.append(_unpack_words(w, 0).astype(BF16))
        los.append(_unpack_words(w, 1).astype(BF16))
    return jnp.concatenate(his + los, axis=1)


def _token_pipeline(idx_hbm, bufs, sems, tab_ref, stages, produce, consume):
    step, n_steps = pl.program_id(0), pl.num_programs(0)
    n_groups = TT // N_STAGE

    def copy(tile, slot):
        cols = pl.ds(pl.multiple_of(tile * TT, TT), TT)
        return pltpu.make_async_copy(idx_hbm.at[:, cols], bufs[slot], sems.at[slot])

    def gather_group(idx_ref, t0):
        for q in range(N_STAGE):
            _gather_token(idx_ref, t0 + q, tab_ref, stages[q])

    def group(row0, refill):
        results = [produce(row0 + q, stages[q]) for q in range(N_STAGE)]
        refill()
        consume(row0, results)

    @pl.when(step == 0)
    def _():
        copy(0, 0).start()
        copy(0, 0).wait()
        gather_group(bufs[0], 0)

    for half in range(TE // TT):
        tile = step * (TE // TT) + half
        cur, nxt = bufs[half], bufs[1 - half]
        base = half * TT

        def body(i, carry, cur=cur, base=base):
            t0 = i * N_STAGE
            group(base + t0, lambda: gather_group(cur, t0 + N_STAGE))
            return carry

        def last_group(nxt=nxt, base=base, tile=tile, half=half):
            copy(tile + 1, 1 - half).wait()
            group(base + TT - N_STAGE, lambda: gather_group(nxt, 0))

        if half == 0:
            copy(tile + 1, 1).start()
            lax.fori_loop(0, n_groups - 1, body, 0)
            last_group()
        else:
            has_next = step + 1 < n_steps

            @pl.when(has_next)
            def _(tile=tile):
                copy(tile + 1, 0).start()

            lax.fori_loop(0, n_groups - 1, body, 0)
            pl.when(has_next)(last_group)

            @pl.when(jnp.logical_not(has_next))
            def _(base=base):
                group(base + TT - N_STAGE, lambda: None)


def _peer_u_kernel(idx_hbm, x_ref, tab_ref, act_ref, xb_ref, act_t_ref, idx0, idx1, sems, *stages):
    xb_ref[...] = x_ref[...].astype(BF16).astype(F32)
    act_t_ref[...] = jnp.zeros((N_SEL, TE), F32)
    lane = lax.broadcasted_iota(jnp.int32, (N_SEL, TE), 1)

    def produce(t, stage):
        xr = xb_ref[pl.ds(t, 1), :]
        p = None
        for j in range(SLABS):
            w = stage[pl.ds(j, N_SEL, stride=SLABS), :]
            term = (_unpack_words(w, 0) * xr[:, j * LANES:(j + 1) * LANES]
                    + _unpack_words(w, 1) * xr[:, WORDS + j * LANES:WORDS + (j + 1) * LANES])
            p = term if p is None else p + term
        return jnp.sum(p, axis=1, keepdims=True)

    def consume(t0, cols):
        acc = act_t_ref[...]
        for q, r in enumerate(cols):
            acc = jnp.where(lane == t0 + q, r, acc)
        act_t_ref[...] = acc

    _token_pipeline(idx_hbm, (idx0, idx1), sems, tab_ref, stages, produce, consume)
    act_ref[...] = act_t_ref[...].T


_INDEX_SCRATCH = [pltpu.SMEM((N_SEL, TT), jnp.int32), pltpu.SMEM((N_SEL, TT), jnp.int32),
                  pltpu.SemaphoreType.DMA((2,))]
_STAGE_SCRATCH = [pltpu.VMEM((SLABS * N_SEL, LANES), jnp.uint32)] * N_STAGE


def _peer_u_call(idx, hf, tab):
    t = hf.shape[0]
    return pl.pallas_call(
        _peer_u_kernel,
        grid=(t // TE,),
        in_specs=[
            pl.BlockSpec(memory_space=pl.ANY),
            pl.BlockSpec((TE, D_MODEL), lambda i: (i, 0)),
            pl.BlockSpec(memory_space=pltpu.VMEM),
        ],
        out_specs=pl.BlockSpec((TE, N_SEL), lambda i: (i, 0)),
        out_shape=jax.ShapeDtypeStruct((t, N_SEL), F32),
        scratch_shapes=[pltpu.VMEM((TE, D_MODEL), F32), pltpu.VMEM((N_SEL, TE), F32)]
        + _INDEX_SCRATCH + _STAGE_SCRATCH,
        compiler_params=_cparams(1),
        name="peer_u",
    )(idx, hf, tab)


def _peer_v_kernel(idx_hbm, act_ref, g_ref, tab_ref, x_ref, gf_ref, lng_ref, lnb_ref, o_ref, w_ref, f_ref,
                   idx0, idx1, sems, *stages):
    w_ref[...] = g_ref[...] * _gelu(act_ref[...])

    def produce(t, stage):
        wr = jnp.broadcast_to(w_ref[pl.ds(t, 1), :], (8, N_SEL)).astype(BF16)
        return jnp.dot(wr, _unpack_stage(stage), preferred_element_type=F32)

    def consume(t0, rows):
        for q, f in enumerate(rows):
            f_ref[pl.ds(t0 + q, 1), :] = f[0:1]

    _token_pipeline(idx_hbm, (idx0, idx1), sems, tab_ref, stages, produce, consume)
    z = DEEPNORM_ALPHA * x_ref[...] + _affine(f_ref[...], gf_ref[0])
    o_ref[...] = _layer_norm(z, lng_ref[...], lnb_ref[...])


def _peer_v_call(idx, act, g, tab, x, gf, lng, lnb, n_prompt_tiles):
    t = x.shape[0]
    sel = lambda i: (jnp.where(i >= n_prompt_tiles, 1, 0), 0, 0)
    row = pl.BlockSpec((1, D_MODEL), lambda i: (0, 0))
    return pl.pallas_call(
        _peer_v_kernel,
        grid=(t // TE,),
        in_specs=[
            pl.BlockSpec(memory_space=pl.ANY),
            pl.BlockSpec((TE, N_SEL), lambda i: (i, 0)),
            pl.BlockSpec((TE, N_SEL), lambda i: (i, 0)),
            pl.BlockSpec(memory_space=pltpu.VMEM),
            pl.BlockSpec((TE, D_MODEL), lambda i: (i, 0)),
            pl.BlockSpec((1, GROUP, D_MODEL), sel),
            row, row,
        ],
        out_specs=pl.BlockSpec((TE, D_MODEL), lambda i: (i, 0)),
        out_shape=jax.ShapeDtypeStruct((t, D_MODEL), F32),
        scratch_shapes=[pltpu.VMEM((TE, N_SEL), F32), pltpu.VMEM((TE, D_MODEL), F32)]
        + _INDEX_SCRATCH + _STAGE_SCRATCH,
        compiler_params=_cparams(1),
        name="peer_v",
    )(idx, act, g, tab, x, gf, lng.reshape(1, D_MODEL), lnb.reshape(1, D_MODEL))


def _time_major(a):
    return jnp.swapaxes(a, 0, 1).reshape(a.shape[0] * a.shape[1], a.shape[2])


def _batch_major(a, nb):
    return jnp.swapaxes(a.reshape(a.shape[0] // nb, nb, a.shape[1]), 0, 1)


def kernel(x_prompt, x_sample, state_lru_h, state_lru_conv, state_conf_conv, c_prompt, c_sample, w_ada, b_ada, ln_mix_g, ln_mix_b, ln_ffn_g, ln_ffn_b, lru_w_in, lru_conv_w, lru_conv_b, lru_w_a, lru_b_a, lru_w_i, lru_b_i, lru_lambda, lru_w_out, conf_w_pw1, conf_b_pw1, conf_dw_w, conf_dw_b, conf_ln_g, conf_ln_b, conf_w_pw2, conf_b_pw2, peer_w_q, peer_keys, peer_u, peer_v):
    bp, lp, _ = x_prompt.shape
    bs, ls, _ = x_sample.shape
    tp, ts = bp * lp, bs * ls
    assert GROUP % bp == 0 and bs == GROUP and TM % GROUP == 0
    for nb, seq in ((bp, lp), (bs, ls)):
        assert (seq * nb) % TM == 0 and (seq * nb == TM or TM // nb >= CONF_CONV_W - 1)
    assert tp % TE == 0 and ts % TE == 0 and TE % GROUP == 0
    npt_m, npt_t = tp // TM, tp // TE

    x = jnp.concatenate([_time_major(x_prompt), _time_major(x_sample)], axis=0)

    mod = _mod_call(jnp.concatenate([c_prompt, c_sample], axis=0), w_ada, b_ada)
    mod = mod.reshape(DEPTH, bp + bs, 6, D_MODEL)
    pat = jnp.stack([jnp.tile(mod[:, :bp], (1, GROUP // bp, 1, 1)), mod[:, bp:]], axis=1)
    pat = jnp.transpose(pat, (0, 3, 1, 2, 4))
    mods = [[pat[i, k] for k in range(6)] for i in range(DEPTH)]

    zeros = lambda *s: jnp.zeros(s, F32)
    new_h, new_lconv, new_cconv = [], [], []
    for i in range(DEPTH):
        sh_m, sc_m, g_m, sh_f, sc_f, g_f = mods[i]
        j = i // 2
        if i % 2 == 0:
            proj = _mm_mod_call(x, sc_m, sh_m, lru_w_in[j].astype(BF16), None, npt_m, False, "lru_in_proj")
            wa, wi = lru_w_a[j].astype(BF16), lru_w_i[j].astype(BF16)
            common = (lru_conv_w[j], lru_conv_b[j], wa, lru_b_a[j], wi, lru_b_i[j], lru_lambda[j])
            yp, hp, cp = _lru_call(proj, zeros((LRU_CONV_W - 1) * bp, D_RNN), zeros(bp, D_RNN), *common,
                                   nb=bp, seq=lp, row_block0=0, name="lru_seq_prompt")
            ys, hs, cs = _lru_call(proj, _time_major(state_lru_conv[j]), state_lru_h[j], *common,
                                   nb=bs, seq=ls, row_block0=npt_m, name="lru_seq_sample")
            new_h.append((hp, hs))
            new_lconv.append((_batch_major(cp, bp), _batch_major(cs, bs)))
            mixed, w_out, b_out = (yp, ys), lru_w_out[j], None
        else:
            u = _mm_mod_call(x, sc_m, sh_m, conf_w_pw1[j].astype(BF16), conf_b_pw1[j], npt_m, True, "conf_pw1_glu")
            common = (conf_dw_w[j], conf_dw_b[j], conf_ln_g[j], conf_ln_b[j])
            dp, cp = _conf_call(u, zeros((CONF_CONV_W - 1) * bp, D_CONF), *common,
                                nb=bp, seq=lp, row_block0=0, name="conf_seq_prompt")
            ds, cs = _conf_call(u, _time_major(state_conf_conv[j]), *common,
                                nb=bs, seq=ls, row_block0=npt_m, name="conf_seq_sample")
            new_cconv.append((_batch_major(cp, bp), _batch_major(cs, bs)))
            mixed, w_out, b_out = (dp, ds), conf_w_pw2[j], conf_b_pw2[j]

        x, hf = _proj_ln_call(*mixed, w_out.astype(BF16), b_out, x, g_m, ln_mix_g[i], ln_mix_b[i], sc_f, sh_f,
                              npt_m, "mix_out_ln")
        idx, gate = _retrieve_call(hf, peer_w_q[i].astype(BF16),
                                   peer_keys[i].astype(BF16).reshape(2 * N_PEER_HEADS, N_KEYS, HALF_KEY))
        act = _peer_u_call(idx, hf, _pack_table(peer_u, i))
        x = _peer_v_call(idx, act, gate, _pack_table(peer_v, i), x, g_f, ln_ffn_g[i], ln_ffn_b[i], npt_t)

    y_prompt = _batch_major(x[:tp], bp)
    y_sample = _batch_major(x[tp:], bs)
    pick = lambda pairs, k: jnp.stack([p[k] for p in pairs])
    return (y_prompt, y_sample, pick(new_h, 0), pick(new_lconv, 0), pick(new_cconv, 0),
            pick(new_h, 1), pick(new_lconv, 1), pick(new_cconv, 1))
```

```python
import functools
import math

import jax
import jax.numpy as jnp
from jax import lax
from jax.experimental import pallas as pl
from jax.experimental.pallas import tpu as pltpu

F32 = jnp.float32
BF16 = jnp.bfloat16

D_MODEL = 1024
DEPTH = 2
D_RNN = 1280
N_RNN_HEADS = 10
RNN_HEAD_DIM = 128
LRU_CONV_W = 4
LRU_C = 8.0
D_CONF = 1024
CONF_CONV_W = 31
N_PEER_HEADS = 8
N_KEYS = 128
N_EXPERTS = N_KEYS * N_KEYS
HALF_KEY = 128
TOPK = 16
N_SEL = N_PEER_HEADS * TOPK
DEEPNORM_ALPHA = (2 * DEPTH) ** 0.25
LN_EPS = 1e-5

LANES = 128
GROUP = 128
TM = 512
TT = 128
WORDS = D_MODEL // 2
SLABS = WORDS // LANES
N_STAGE = 4
VMEM_LIMIT = 56 * 1024 * 1024

_NT = (((1,), (1,)), ((), ()))


def _cparams(n_axes, vmem=VMEM_LIMIT):
    return pltpu.CompilerParams(dimension_semantics=("arbitrary",) * n_axes, vmem_limit_bytes=vmem)


def _gelu(x):
    return 0.5 * x * (1.0 + lax.erf(x * (1.0 / math.sqrt(2.0))))


def _affine(x, scale, shift=None):
    rows, c = x.shape
    xr = x.reshape(rows // GROUP, GROUP, c) * scale[None]
    if shift is not None:
        xr = xr + shift[None]
    return xr.reshape(rows, c)


def _layer_norm(z, g, b):
    mu = jnp.mean(z, axis=-1, keepdims=True)
    zc = z - mu
    var = jnp.mean(zc * zc, axis=-1, keepdims=True)
    return zc * lax.rsqrt(var + LN_EPS) * g + b


def _mod_kernel(c_ref, w_ref, b_ref, o_ref):
    c = c_ref[...]
    a = (c * jax.nn.sigmoid(c)).astype(BF16)
    o_ref[0] = jnp.dot(a, w_ref[0].astype(BF16), preferred_element_type=F32) + b_ref[0]


def _mod_call(c_all, w_ada, b_ada):
    nb = c_all.shape[0]
    tn = 1536
    return pl.pallas_call(
        _mod_kernel,
        grid=(DEPTH, 6 * D_MODEL // tn),
        in_specs=[
            pl.BlockSpec((nb, D_MODEL), lambda l, j: (0, 0)),
            pl.BlockSpec((1, D_MODEL, tn), lambda l, j: (l, 0, j)),
            pl.BlockSpec((1, 1, tn), lambda l, j: (l, 0, j)),
        ],
        out_specs=pl.BlockSpec((1, nb, tn), lambda l, j: (l, 0, j)),
        out_shape=jax.ShapeDtypeStruct((DEPTH, nb, 6 * D_MODEL), F32),
        compiler_params=_cparams(2),
        name="adaln_mod",
    )(c_all, w_ada, b_ada.reshape(DEPTH, 1, 6 * D_MODEL))


def _mm_mod_kernel(x_ref, sc_ref, sh_ref, w_ref, *rest, glu):
    o_ref = rest[-1]
    xm = _affine(x_ref[...], 1.0 + sc_ref[0], sh_ref[0])
    p = jnp.dot(xm.astype(BF16), w_ref[...], preferred_element_type=F32)
    if len(rest) == 2:
        p = p + rest[0][...]
    if glu:
        n = p.shape[-1] // 2
        p = p[:, :n] * jax.nn.sigmoid(p[:, n:])
    o_ref[...] = p


def _mm_mod_call(x, sc, sh, w_bf16, bias, n_prompt_tiles, glu, name):
    t = x.shape[0]
    n = w_bf16.shape[1]
    n_out = n // 2 if glu else n
    sel = lambda i: (jnp.where(i >= n_prompt_tiles, 1, 0), 0, 0)
    in_specs = [
        pl.BlockSpec((TM, D_MODEL), lambda i: (i, 0)),
        pl.BlockSpec((1, GROUP, D_MODEL), sel),
        pl.BlockSpec((1, GROUP, D_MODEL), sel),
        pl.BlockSpec((D_MODEL, n), lambda i: (0, 0)),
    ]
    args = [x, sc, sh, w_bf16]
    if bias is not None:
        in_specs.append(pl.BlockSpec((1, n), lambda i: (0, 0)))
        args.append(bias.reshape(1, n))
    return pl.pallas_call(
        functools.partial(_mm_mod_kernel, glu=glu),
        grid=(t // TM,),
        in_specs=in_specs,
        out_specs=pl.BlockSpec((TM, n_out), lambda i: (i, 0)),
        out_shape=jax.ShapeDtypeStruct((t, n_out), F32),
        compiler_params=_cparams(1),
        name=name,
    )(*args)


def _lru_kernel(proj_ref, conv0_ref, h0_ref, cw_ref, cb_ref, wa_ref, ba_ref, wi_ref, bi_ref, lam_ref,
                yg_ref, hnew_ref, convnew_ref, win_ref, xc_ref, a_ref, b_ref, h_ref, *, nb, tl, cw, n_tiles):
    rows = tl * nb
    halo = (LRU_CONV_W - 1) * nb

    @pl.when(pl.program_id(0) == 0)
    def _():
        win_ref[0:halo, :] = conv0_ref[...]
        h_ref[...] = h0_ref[...]

    win_ref[halo:halo + rows, :] = proj_ref[:, :D_RNN]
    xc = cb_ref[...] + win_ref[0:rows, :] * cw_ref[0:1, :]
    for w in range(1, LRU_CONV_W):
        xc = xc + win_ref[w * nb:w * nb + rows, :] * cw_ref[w:w + 1, :]
    xc_ref[...] = xc
    carry = win_ref[rows:rows + halo, :]
    convnew_ref[...] = carry
    if n_tiles > 1:
        win_ref[0:halo, :] = carry

    lam = lam_ref[...]
    neg = -lam
    softplus = jnp.maximum(neg, 0.0) + jnp.log1p(jnp.exp(-jnp.abs(neg)))
    for hd in range(N_RNN_HEADS):
        cs = slice(hd * RNN_HEAD_DIM, (hd + 1) * RNN_HEAD_DIM)
        xh = xc_ref[:, cs]
        xb = xh.astype(BF16)
        r = jax.nn.sigmoid(jnp.dot(xb, wa_ref[hd], preferred_element_type=F32) + ba_ref[:, cs])
        ig = jax.nn.sigmoid(jnp.dot(xb, wi_ref[hd], preferred_element_type=F32) + bi_ref[:, cs])
        log_a = (-LRU_C * r) * softplus[:, cs]
        th = jnp.tanh(log_a)
        a_ref[:, cs] = jnp.exp(log_a)
        b_ref[:, cs] = jnp.sqrt(-2.0 * th / (1.0 - th)) * (ig * xh)

    for c0 in range(0, D_RNN, cw):
        cs = slice(c0, c0 + cw)

        def step(l, h, cs=cs):
            rs = pl.ds(pl.multiple_of(l * nb, nb), nb)
            h = a_ref[rs, cs] * h + b_ref[rs, cs]
            b_ref[rs, cs] = h
            return h

        h_ref[:, cs] = lax.fori_loop(0, tl, step, h_ref[:, cs])

    yg_ref[...] = b_ref[...] * _gelu(proj_ref[:, D_RNN:])
    hnew_ref[...] = h_ref[...]


def _lru_call(proj, conv0, h0, cw, cb, wa, ba, wi, bi, lam, *, nb, seq, row_block0, name):
    tl = TM // nb
    n_tiles = seq // tl
    halo = (LRU_CONV_W - 1) * nb
    chan = D_RNN if nb * D_RNN <= 16 * 1024 else LANES
    full = lambda shape: pl.BlockSpec(shape, lambda i: (0,) * len(shape))
    return pl.pallas_call(
        functools.partial(_lru_kernel, nb=nb, tl=tl, cw=chan, n_tiles=n_tiles),
        grid=(n_tiles,),
        in_specs=[
            pl.BlockSpec((TM, 2 * D_RNN), lambda i: (row_block0 + i, 0)),
            full((halo, D_RNN)),
            full((nb, D_RNN)),
            full((LRU_CONV_W, D_RNN)),
            full((1, D_RNN)),
            full((N_RNN_HEADS, RNN_HEAD_DIM, RNN_HEAD_DIM)),
            full((1, D_RNN)),
            full((N_RNN_HEADS, RNN_HEAD_DIM, RNN_HEAD_DIM)),
            full((1, D_RNN)),
            full((1, D_RNN)),
        ],
        out_specs=[
            pl.BlockSpec((TM, D_RNN), lambda i: (i, 0)),
            full((nb, D_RNN)),
            full((halo, D_RNN)),
        ],
        out_shape=[
            jax.ShapeDtypeStruct((seq * nb, D_RNN), F32),
            jax.ShapeDtypeStruct((nb, D_RNN), F32),
            jax.ShapeDtypeStruct((halo, D_RNN), F32),
        ],
        scratch_shapes=[
            pltpu.VMEM((halo + TM, D_RNN), F32),
            pltpu.VMEM((TM, D_RNN), F32),
            pltpu.VMEM((TM, D_RNN), F32),
            pltpu.VMEM((TM, D_RNN), F32),
            pltpu.VMEM((nb, D_RNN), F32),
        ],
        compiler_params=_cparams(1),
        name=name,
    )(proj, conv0, h0, cw, cb.reshape(1, D_RNN), wa, ba.reshape(1, D_RNN), wi, bi.reshape(1, D_RNN),
      lam.reshape(1, D_RNN))


def _conf_kernel(u_ref, conv0_ref, dw_ref, db_ref, g_ref, b_ref, o_ref, convnew_ref, win_ref, d_ref,
                 *, nb, tl, n_tiles):
    rows = tl * nb
    halo = (CONF_CONV_W - 1) * nb
    rc = 64
    cc = 256

    @pl.when(pl.program_id(0) == 0)
    def _():
        win_ref[0:halo, :] = conv0_ref[...]

    win_ref[halo:halo + rows, :] = u_ref[...]

    for c0 in range(0, D_CONF, cc):
        cs = slice(c0, c0 + cc)

        def chunk(r, carry, cs=cs):
            r0 = pl.multiple_of(r * rc, rc)
            acc = jnp.broadcast_to(db_ref[:, cs], (rc, cc))
            for w in range(CONF_CONV_W):
                acc = acc + win_ref[pl.ds(r0 + w * nb, rc), cs] * dw_ref[w:w + 1, cs]
            d_ref[pl.ds(r0, rc), cs] = acc
            return carry

        lax.fori_loop(0, rows // rc, chunk, 0)

    convnew_ref[...] = win_ref[rows:rows + halo, :]
    if n_tiles > 1:
        win_ref[0:halo, :] = win_ref[rows:rows + halo, :]

    y = _layer_norm(d_ref[...], g_ref[...], b_ref[...])
    o_ref[...] = y * jax.nn.sigmoid(y)


def _conf_call(u, conv0, dw, db, g, b, *, nb, seq, row_block0, name):
    tl = TM // nb
    n_tiles = seq // tl
    halo = (CONF_CONV_W - 1) * nb
    full = lambda shape: pl.BlockSpec(shape, lambda i: (0,) * len(shape))
    return pl.pallas_call(
        functools.partial(_conf_kernel, nb=nb, tl=tl, n_tiles=n_tiles),
        grid=(n_tiles,),
        in_specs=[
            pl.BlockSpec((TM, D_CONF), lambda i: (row_block0 + i, 0)),
            full((halo, D_CONF)),
            full((CONF_CONV_W, D_CONF)),
            full((1, D_CONF)),
            full((1, D_CONF)),
            full((1, D_CONF)),
        ],
        out_specs=[
            pl.BlockSpec((TM, D_CONF), lambda i: (i, 0)),
            full((halo, D_CONF)),
        ],
        out_shape=[
            jax.ShapeDtypeStruct((seq * nb, D_CONF), F32),
            jax.ShapeDtypeStruct((halo, D_CONF), F32),
        ],
        scratch_shapes=[
            pltpu.VMEM((halo + TM, D_CONF), F32),
            pltpu.VMEM((TM, D_CONF), F32),
        ],
        compiler_params=_cparams(1),
        name=name,
    )(u, conv0, dw, db.reshape(1, D_CONF), g.reshape(1, D_CONF), b.reshape(1, D_CONF))


def _proj_ln_kernel(yp_ref, ys_ref, w_ref, x_ref, gm_ref, lng_ref, lnb_ref, scf_ref, shf_ref, *rest, n_prompt_tiles):
    x1_ref, hf_ref = rest[-2:]
    y = jnp.where(pl.program_id(0) < n_prompt_tiles, yp_ref[...], ys_ref[...])
    out = jnp.dot(y.astype(BF16), w_ref[...], preferred_element_type=F32)
    if len(rest) == 3:
        out = out + rest[0][...]
    z = DEEPNORM_ALPHA * x_ref[...] + _affine(out, gm_ref[0])
    x1 = _layer_norm(z, lng_ref[...], lnb_ref[...])
    x1_ref[...] = x1
    hf_ref[...] = _affine(x1, 1.0 + scf_ref[0], shf_ref[0]).astype(BF16)


def _proj_ln_call(y_prompt, y_sample, w_bf16, bias, x, gm, lng, lnb, scf, shf, n_prompt_tiles, name):
    t, k = x.shape[0], y_prompt.shape[1]
    sel = lambda i: (jnp.where(i >= n_prompt_tiles, 1, 0), 0, 0)
    row = pl.BlockSpec((1, D_MODEL), lambda i: (0, 0))
    mod = pl.BlockSpec((1, GROUP, D_MODEL), sel)
    tile = pl.BlockSpec((TM, D_MODEL), lambda i: (i, 0))
    in_specs = [
        pl.BlockSpec((TM, k), lambda i: (jnp.minimum(i, n_prompt_tiles - 1), 0)),
        pl.BlockSpec((TM, k), lambda i: (jnp.maximum(i - n_prompt_tiles, 0), 0)),
        pl.BlockSpec((k, D_MODEL), lambda i: (0, 0)),
        tile, mod, row, row, mod, mod,
    ]
    args = [y_prompt, y_sample, w_bf16, x, gm, lng.reshape(1, D_MODEL), lnb.reshape(1, D_MODEL), scf, shf]
    if bias is not None:
        in_specs.append(row)
        args.append(bias.reshape(1, D_MODEL))
    return pl.pallas_call(
        functools.partial(_proj_ln_kernel, n_prompt_tiles=n_prompt_tiles),
        grid=(t // TM,),
        in_specs=in_specs,
        out_specs=[tile, tile],
        out_shape=[jax.ShapeDtypeStruct((t, D_MODEL), F32), jax.ShapeDtypeStruct((t, D_MODEL), BF16)],
        compiler_params=_cparams(1),
        name=name,
    )(*args)


SUB = 8

_CELLS = [(i, j) for i in range(TOPK) for j in range(TOPK) if (i + 1) * (j + 1) <= TOPK]
_CELLS = _CELLS + [None] * (SUB * pl.next_power_of_2(pl.cdiv(len(_CELLS), SUB)) - len(_CELLS))
_CELL_GROUPS = [_CELLS[k:k + SUB] for k in range(0, len(_CELLS), SUB)]


def _stack_rows(rows, r_iota):
    out = jnp.broadcast_to(rows[-1], r_iota.shape)
    for r in range(SUB - 2, -1, -1):
        if rows[r] is not rows[r + 1]:
            out = jnp.where(r_iota <= r, rows[r], out)
    return out


def _row_iota(shape):
    return lax.broadcasted_iota(jnp.int32, shape, 0).astype(F32)


def _sort_network(lo, hi):
    def merge(lo, hi, r):
        step = 2 * r
        if step < hi - lo:
            yield from merge(lo, hi, step)
            yield from merge(lo + r, hi, step)
            yield from ((i, i + r) for i in range(lo + r, hi - r, step))
        else:
            yield (lo, lo + r)

    if hi > lo:
        mid = lo + (hi - lo) // 2
        yield from _sort_network(lo, mid)
        yield from _sort_network(mid + 1, hi)
        yield from merge(lo, hi, 1)


def _top_rows(s, k, tag=None):
    n = s.shape[0]
    depth = n // SUB
    iota = _row_iota(s.shape)
    split = lambda a: [a[v * SUB:(v + 1) * SUB, :] for v in range(depth)]
    cols = [split(-s), split(iota)] + ([] if tag is None else [split(tag)])
    neg, idx = cols[0], cols[1]
    for a, b in _sort_network(0, depth - 1):
        tie = neg[a] == neg[b]
        swap = jnp.where(tie, idx[b], neg[b]) < jnp.where(tie, idx[a], neg[a])
        for c in cols:
            c[a], c[b] = jnp.where(swap, c[b], c[a]), jnp.where(swap, c[a], c[b])
    vals, tags = [], []
    for r in range(k):
        m = jnp.min(neg[0], axis=0, keepdims=True)
        am = jnp.min(jnp.where(neg[0] == m, idx[0], float(n)), axis=0, keepdims=True)
        won = idx[0] == am
        vals.append(-m)
        tags.append(am if tag is None else jnp.max(jnp.where(won, cols[2][0], -1.0), axis=0, keepdims=True))
        for d in range(min(depth - 1, k - 1 - r)):
            for c in cols:
                c[d] = jnp.where(won, c[d + 1], c[d])
    return vals, tags


def _retrieve_kernel(hf_ref, wq_ref, keys_ref, idx_ref, g_ref, g_t_ref):
    q = jnp.dot(hf_ref[...].astype(BF16), wq_ref[...], preferred_element_type=F32).astype(BF16)
    r_iota = lax.broadcasted_iota(jnp.int32, (SUB, TT), 0)
    neg_inf = jnp.full((1, TT), -jnp.inf, F32)
    zero = jnp.zeros((1, TT), F32)
    for h in range(N_PEER_HEADS):
        halves = []
        for p in range(2):
            hp = h * 2 + p
            s = lax.dot_general(keys_ref[hp], q[:, hp * HALF_KEY:(hp + 1) * HALF_KEY], _NT,
                                preferred_element_type=F32)
            halves.append(_top_rows(s, TOPK))
        (v1, i1), (v2, i2) = halves
        i1 = [iv * float(N_KEYS) for iv in i1]
        cand, cidx = [], []
        for cells in _CELL_GROUPS:
            pick = lambda rows, pos, pad: _stack_rows([pad if c is None else rows[c[pos]] for c in cells], r_iota)
            cand.append(pick(v1, 0, neg_inf) + pick(v2, 1, neg_inf))
            cidx.append(pick(i1, 0, zero) + pick(i2, 1, zero))
        cand = jnp.concatenate(cand, axis=0)
        cidx = jnp.concatenate(cidx, axis=0)
        top_s, experts = _top_rows(cand, TOPK, tag=cidx)
        e =[jnp.exp(_stack_rows(top_s[k:k + SUB], r_iota) - top_s[0]) for k in range(0, TOPK, SUB)]
        denom = jnp.sum(sum(e[1:], e[0]), axis=0, keepdims=True)
        for k, ev in enumerate(e):
            rows = slice(h * TOPK + k * SUB, h * TOPK + (k + 1) * SUB)
            g_t_ref[rows, :] = ev / denom
            ids = _stack_rows(experts[k * SUB:(k + 1) * SUB], r_iota)
            idx_ref[rows, :] = ids.astype(jnp.int32) * SLABS
    g_ref[...] = g_t_ref[...].T


def _retrieve_call(hf, wq_bf16, keys_bf16):
    t = hf.shape[0]
    nq = wq_bf16.shape[1]
    return pl.pallas_call(
        _retrieve_kernel,
        grid=(t // TT,),
        in_specs=[
            pl.BlockSpec((TT, D_MODEL), lambda i: (i, 0)),
            pl.BlockSpec((D_MODEL, nq), lambda i: (0, 0)),
            pl.BlockSpec((2 * N_PEER_HEADS, N_KEYS, HALF_KEY), lambda i: (0, 0, 0)),
        ],
        out_specs=[
            pl.BlockSpec((N_SEL, TT), lambda i: (0, i)),
            pl.BlockSpec((TT, N_SEL), lambda i: (i, 0)),
        ],
        out_shape=[
            jax.ShapeDtypeStruct((N_SEL, t), jnp.int32),
            jax.ShapeDtypeStruct((t, N_SEL), F32),
        ],
        scratch_shapes=[pltpu.VMEM((N_SEL, TT), F32)],
        compiler_params=_cparams(1),
        name="peer_retrieve",
    )(hf, wq_bf16, keys_bf16)


def _pack_kernel(t_ref, o_ref):
    x = t_ref[0]
    w = pltpu.pack_elementwise([x[:, :WORDS], x[:, WORDS:]], packed_dtype=BF16)
    for j in range(SLABS):
        o_ref[pl.ds(j, TM, stride=SLABS), :] = w[:, j * LANES:(j + 1) * LANES]


def _pack_table(tabs, layer):
    return pl.pallas_call(
        _pack_kernel,
        grid=(N_EXPERTS // TM,),
        in_specs=[pl.BlockSpec((1, TM, D_MODEL), lambda i: (layer, i, 0))],
        out_specs=pl.BlockSpec((TM * SLABS, LANES), lambda i: (i, 0)),
        out_shape=jax.ShapeDtypeStruct((N_EXPERTS * SLABS, LANES), jnp.uint32),
        compiler_params=_cparams(1),
        name="peer_pack_table",
    )(tabs)


def _gather_token(idx_ref, t, tab_ref, stage_ref):
    for k in range(N_SEL):
        i = pl.multiple_of(idx_ref.at[k][t], SLABS)
        stage_ref[pl.ds(k * SLABS, SLABS), :] = tab_ref[pl.ds(i, SLABS), :]


def _unpack_words(w, half):
    return pltpu.unpack_elementwise(w, index=half, packed_dtype=BF16, unpacked_dtype=F32)


def _unpack_stage(stage_ref):
    his, los = [], []
    for j in range(SLABS):
        w = stage_ref[pl.ds(j, N_SEL, stride=SLABS), :]
        his.append(_unpack_words(w, 0).astype(BF16))
        los.append(_unpack_words(w, 1).astype(BF16))
    return jnp.concatenate(his + los, axis=1)


def _token_pipeline(idx_hbm, bufs, sems, tab_ref, stages, produce, consume):
    step, n_steps = pl.program_id(0), pl.num_programs(0)
    n_groups = TT // N_STAGE

    def copy(tile, slot):
        cols = pl.ds(pl.multiple_of(tile * TT, TT), TT)
        return pltpu.make_async_copy(idx_hbm.at[:, cols], bufs[slot], sems.at[slot])

    def gather_group(idx_ref, t0):
        for q in range(N_STAGE):
            _gather_token(idx_ref, t0 + q, tab_ref, stages[q])

    def group(t0, refill):
        results = [produce(t0 + q, stages[q]) for q in range(N_STAGE)]
        refill()
        consume(t0, results)

    @pl.when(step == 0)
    def _():
        copy(0, 0).start()
        copy(0, 0).wait()
        gather_group(bufs[0], 0)

    for slot in range(2):
        @pl.when(lax.rem(step, 2) == slot)
        def _(slot=slot):
            has_next = step + 1 < n_steps

            @pl.when(has_next)
            def _():
                copy(step + 1, 1 - slot).start()

            def body(i, carry):
                t0 = i * N_STAGE
                group(t0, lambda: gather_group(bufs[slot], t0 + N_STAGE))
                return carry

            lax.fori_loop(0, n_groups - 1, body, 0)

            @pl.when(has_next)
            def _():
                copy(step + 1, 1 - slot).wait()
                group(TT - N_STAGE, lambda: gather_group(bufs[1 - slot], 0))

            @pl.when(jnp.logical_not(has_next))
            def _():
                group(TT - N_STAGE, lambda: None)


def _peer_u_kernel(idx_hbm, x_ref, tab_ref, act_ref, xb_ref, act_t_ref, idx0, idx1, sems, *stages):
    xb_ref[...] = x_ref[...].astype(BF16).astype(F32)
    act_t_ref[...] = jnp.zeros((N_SEL, TT), F32)
    lane = lax.broadcasted_iota(jnp.int32, (N_SEL, TT), 1)

    def produce(t, stage):
        xr = xb_ref[pl.ds(t, 1), :]
        p = None
        for j in range(SLABS):
            w = stage[pl.ds(j, N_SEL, stride=SLABS), :]
            term = (_unpack_words(w, 0) * xr[:, j * LANES:(j + 1) * LANES]
                    + _unpack_words(w, 1) * xr[:, WORDS + j * LANES:WORDS + (j + 1) * LANES])
            p = term if p is None else p + term
        return jnp.sum(p, axis=1, keepdims=True)

    def consume(t0, cols):
        acc = act_t_ref[...]
        for q, r in enumerate(cols):
            acc = jnp.where(lane == t0 + q, r, acc)
        act_t_ref[...] = acc

    _token_pipeline(idx_hbm, (idx0, idx1), sems, tab_ref, stages, produce, consume)
    act_ref[...] = act_t_ref[...].T


_INDEX_SCRATCH = [pltpu.SMEM((N_SEL, TT), jnp.int32), pltpu.SMEM((N_SEL, TT), jnp.int32),
                  pltpu.SemaphoreType.DMA((2,))]
_STAGE_SCRATCH = [pltpu.VMEM((SLABS * N_SEL, LANES), jnp.uint32)] * N_STAGE


def _peer_u_call(idx, hf, tab):
    t = hf.shape[0]
    return pl.pallas_call(
        _peer_u_kernel,
        grid=(t // TT,),
        in_specs=[
            pl.BlockSpec(memory_space=pl.ANY),
            pl.BlockSpec((TT, D_MODEL), lambda i: (i, 0)),
            pl.BlockSpec(memory_space=pltpu.VMEM),
        ],
        out_specs=pl.BlockSpec((TT, N_SEL), lambda i: (i, 0)),
        out_shape=jax.ShapeDtypeStruct((t, N_SEL), F32),
        scratch_shapes=[pltpu.VMEM((TT, D_MODEL), F32), pltpu.VMEM((N_SEL, TT), F32)]
        + _INDEX_SCRATCH + _STAGE_SCRATCH,
        compiler_params=_cparams(1),
        name="peer_u",
    )(idx, hf, tab)


def _peer_v_kernel(idx_hbm, act_ref, g_ref, tab_ref, x_ref, gf_ref, lng_ref, lnb_ref, o_ref, w_ref, f_ref,
                   idx0, idx1, sems, *stages):
    w_ref[...] = g_ref[...] * _gelu(act_ref[...])

    def produce(t, stage):
        wr = jnp.broadcast_to(w_ref[pl.ds(t, 1), :], (8, N_SEL)).astype(BF16)
        return jnp.dot(wr, _unpack_stage(stage), preferred_element_type=F32)

    def consume(t0, rows):
        for q, f in enumerate(rows):
            f_ref[pl.ds(t0 + q, 1), :] = f[0:1]

    _token_pipeline(idx_hbm, (idx0, idx1), sems, tab_ref, stages, produce, consume)
    z = DEEPNORM_ALPHA * x_ref[...] + gf_ref[0] * f_ref[...]
    o_ref[...] = _layer_norm(z, lng_ref[...], lnb_ref[...])


def _peer_v_call(idx, act, g, tab, x, gf, lng, lnb, n_prompt_tiles):
    t = x.shape[0]
    sel = lambda i: (jnp.where(i >= n_prompt_tiles, 1, 0), 0, 0)
    row = pl.BlockSpec((1, D_MODEL), lambda i: (0, 0))
    return pl.pallas_call(
        _peer_v_kernel,
        grid=(t // TT,),
        in_specs=[
            pl.BlockSpec(memory_space=pl.ANY),
            pl.BlockSpec((TT, N_SEL), lambda i: (i, 0)),
            pl.BlockSpec((TT, N_SEL), lambda i: (i, 0)),
            pl.BlockSpec(memory_space=pltpu.VMEM),
            pl.BlockSpec((TT, D_MODEL), lambda i: (i, 0)),
            pl.BlockSpec((1, GROUP, D_MODEL), sel),
            row, row,
        ],
        out_specs=pl.BlockSpec((TT, D_MODEL), lambda i: (i, 0)),
        out_shape=jax.ShapeDtypeStruct((t, D_MODEL), F32),
        scratch_shapes=[pltpu.VMEM((TT, N_SEL), F32), pltpu.VMEM((TT, D_MODEL), F32)]
        + _INDEX_SCRATCH + _STAGE_SCRATCH,
        compiler_params=_cparams(1),
        name="peer_v",
    )(idx, act, g, tab, x, gf, lng.reshape(1, D_MODEL), lnb.reshape(1, D_MODEL))


def _time_major(a):
    return jnp.swapaxes(a, 0, 1).reshape(a.shape[0] * a.shape[1], a.shape[2])


def _batch_major(a, nb):
    return jnp.swapaxes(a.reshape(a.shape[0] // nb, nb, a.shape[1]), 0, 1)


def kernel(x_prompt, x_sample, state_lru_h, state_lru_conv, state_conf_conv, c_prompt, c_sample, w_ada, b_ada, ln_mix_g, ln_mix_b, ln_ffn_g, ln_ffn_b, lru_w_in, lru_conv_w, lru_conv_b, lru_w_a, lru_b_a, lru_w_i, lru_b_i, lru_lambda, lru_w_out, conf_w_pw1, conf_b_pw1, conf_dw_w, conf_dw_b, conf_ln_g, conf_ln_b, conf_w_pw2, conf_b_pw2, peer_w_q, peer_keys, peer_u, peer_v):
    bp, lp, _ = x_prompt.shape
    bs, ls, _ = x_sample.shape
    tp, ts = bp * lp, bs * ls
    assert GROUP % bp == 0 and bs == GROUP and TM % GROUP == 0
    for nb, seq in ((bp, lp), (bs, ls)):
        assert (seq * nb) % TM == 0 and (seq * nb == TM or TM // nb >= CONF_CONV_W - 1)
    npt_m, npt_t = tp // TM, tp // TT

    x = jnp.concatenate([_time_major(x_prompt), _time_major(x_sample)], axis=0)

    mod = _mod_call(jnp.concatenate([c_prompt, c_sample], axis=0), w_ada, b_ada)
    mod = mod.reshape(DEPTH, bp + bs, 6, D_MODEL)
    pat = jnp.stack([jnp.tile(mod[:, :bp], (1, GROUP // bp, 1, 1)), mod[:, bp:]], axis=1)
    pat = jnp.transpose(pat, (0, 3, 1, 2, 4))
    mods = [[pat[i, k] for k in range(6)] for i in range(DEPTH)]

    zeros = lambda *s: jnp.zeros(s, F32)
    new_h, new_lconv, new_cconv = [], [], []
    for i in range(DEPTH):
        sh_m, sc_m, g_m, sh_f, sc_f, g_f = mods[i]
        j = i // 2
        if i % 2 == 0:
            proj = _mm_mod_call(x, sc_m, sh_m, lru_w_in[j].astype(BF16), None, npt_m, False, "lru_in_proj")
            wa, wi = lru_w_a[j].astype(BF16), lru_w_i[j].astype(BF16)
            common = (lru_conv_w[j], lru_conv_b[j], wa, lru_b_a[j], wi, lru_b_i[j], lru_lambda[j])
            yp, hp, cp = _lru_call(proj, zeros((LRU_CONV_W - 1) * bp, D_RNN), zeros(bp, D_RNN), *common,
                                   nb=bp, seq=lp, row_block0=0, name="lru_seq_prompt")
            ys, hs, cs = _lru_call(proj, _time_major(state_lru_conv[j]), state_lru_h[j], *common,
                                   nb=bs, seq=ls, row_block0=npt_m, name="lru_seq_sample")
            new_h.append((hp, hs))
            new_lconv.append((_batch_major(cp, bp), _batch_major(cs, bs)))
            mixed, w_out, b_out = (yp, ys), lru_w_out[j], None
        else:
            u = _mm_mod_call(x, sc_m, sh_m, conf_w_pw1[j].astype(BF16), conf_b_pw1[j], npt_m, True, "conf_pw1_glu")
            common = (conf_dw_w[j], conf_dw_b[j], conf_ln_g[j], conf_ln_b[j])
            dp, cp = _conf_call(u, zeros((CONF_CONV_W - 1) * bp, D_CONF), *common,
                                nb=bp, seq=lp, row_block0=0, name="conf_seq_prompt")
            ds, cs = _conf_call(u, _time_major(state_conf_conv[j]), *common,
                                nb=bs, seq=ls, row_block0=npt_m, name="conf_seq_sample")
            new_cconv.append((_batch_major(cp, bp), _batch_major(cs, bs)))
            mixed, w_out, b_out = (dp, ds), conf_w_pw2[j], conf_b_pw2[j]

        x, hf = _proj_ln_call(*mixed, w_out.astype(BF16), b_out, x, g_m, ln_mix_g[i], ln_mix_b[i], sc_f, sh_f,
                              npt_m, "mix_out_ln")
        idx, gate = _retrieve_call(hf, peer_w_q[i].astype(BF16),
                                   peer_keys[i].astype(BF16).reshape(2 * N_PEER_HEADS, N_KEYS, HALF_KEY))
        act = _peer_u_call(idx, hf, _pack_table(peer_u, i))
        x = _peer_v_call(idx, act, gate, _pack_table(peer_v, i), x, g_f, ln_ffn_g[i], ln_ffn_b[i], npt_t)

    y_prompt = _batch_major(x[:tp], bp)
    y_sample = _batch_major(x[tp:], bs)
    pick = lambda pairs, k: jnp.stack([p[k] for p in pairs])
    return (y_prompt, y_sample, pick(new_h, 0), pick(new_lconv, 0), pick(new_cconv, 0),
            pick(new_h, 1), pick(new_lconv, 1), pick(new_cconv, 1))
```
